```python
import jax, jax.numpy as jnp
from jax import lax
import numpy as np

D_MODEL = 2048
BATCH = 8
SEQ = 2048
DEPTH = 2

HEAD_DIM = 128
RET_WIDTH = D_MODEL // 2
N_RET_HEADS = RET_WIDTH // HEAD_DIM
FOX_WIDTH = D_MODEL // 2
N_FOX_HEADS = FOX_WIDTH // HEAD_DIM
POOL_WIDTH = D_MODEL // 2
POOL_WINDOWS = (2, 4, 8, 16)
N_POOL_GROUPS = len(POOL_WINDOWS)
POOL_GROUP_DIM = POOL_WIDTH // N_POOL_GROUPS
N_BRANCH = 3
RET_CHUNK = 128
Q_BLOCK = 128
ROPE_THETA = 10000.0
EPS = 1e-6
NEG_INF = -1e30
C_IN = 4 * RET_WIDTH + 4 * FOX_WIDTH + N_FOX_HEADS + 2 * POOL_WIDTH + N_BRANCH * D_MODEL

kernel_name = 'retention_fox_pool_gated_hybrid'


def rms_norm(x, g):
    xf = x.astype(jnp.float32)
    y = xf * lax.rsqrt(jnp.mean(xf * xf, axis=-1, keepdims=True) + EPS)
    return (y * g.astype(jnp.float32)).astype(x.dtype)


def rotary(t, pos):
    half = t.shape[-1] // 2
    inv = ROPE_THETA ** (-jnp.arange(half, dtype=jnp.float32) / half)
    ang = pos.astype(jnp.float32)[:, None] * inv[None, :]
    cos = jnp.cos(ang)[None, :, None, :]
    sin = jnp.sin(ang)[None, :, None, :]
    t1, t2 = t[..., :half], t[..., half:]
    return jnp.concatenate([t1 * cos - t2 * sin, t1 * sin + t2 * cos], axis=-1)


def retention(q, k, v):
    bsz, s, h, dk = q.shape
    dv = v.shape[-1]
    n = s // RET_CHUNK
    pos = jnp.arange(s)
    q = rotary(q, pos)
    k = rotary(k, pos) * (dk ** -0.5)
    log_g = jnp.log1p(-jnp.exp2(-5.0 - jnp.arange(h, dtype=jnp.float32)))
    idx = jnp.arange(RET_CHUNK, dtype=jnp.float32)
    diff = idx[:, None] - idx[None, :]
    decay = jnp.where(diff >= 0, jnp.exp(log_g[:, None, None] * jnp.maximum(diff, 0.0)), 0.0)
    xi = jnp.exp(log_g[:, None] * (idx + 1.0))[None, :, :, None]
    zeta = jnp.exp(log_g[:, None] * (RET_CHUNK - 1.0 - idx))[None, :, :, None]
    g_chunk = jnp.exp(log_g * RET_CHUNK)[None, :, None, None]

    def chunks(t):
        return t.reshape(bsz, n, RET_CHUNK, h, t.shape[-1]).transpose(1, 0, 3, 2, 4)

    def step(state, qkv):
        qc, kc, vc = qkv
        inner = jnp.einsum('bhqd,bhkd->bhqk', qc, kc) * decay
        out = (jnp.einsum('bhqk,bhkv->bhqv', inner, vc)
               + jnp.einsum('bhqd,bhdv->bhqv', qc, state) * xi)
        state = g_chunk * state + jnp.einsum('bhkd,bhkv->bhdv', kc * zeta, vc)
        return state, out

    state0 = jnp.zeros((bsz, h, dk, dv), jnp.float32)
    _, outs = lax.scan(step, state0, (chunks(q), chunks(k), chunks(v)))
    return outs.transpose(1, 0, 3, 2, 4).reshape(bsz, s, h, dv)


def head_group_norm(y, g):
    mu = jnp.mean(y, axis=-1, keepdims=True)
    var = jnp.mean(jnp.square(y - mu), axis=-1, keepdims=True)
    yn = (y - mu) * lax.rsqrt(var + EPS)
    return yn * g.astype(jnp.float32).reshape(y.shape[-2], y.shape[-1])


def forgetting_attention(q, k, v, f_logit):
    bsz, s, h, d = q.shape
    n = s // Q_BLOCK
    scale = d ** -0.5
    c = jnp.cumsum(jax.nn.log_sigmoid(f_logit), axis=1).transpose(0, 2, 1)
    qh = q.transpose(0, 2, 1, 3)
    kh = k.transpose(0, 2, 1, 3)
    vh = v.transpose(0, 2, 1, 3)
    qb = qh.reshape(bsz, h, n, Q_BLOCK, d).transpose(2, 0, 1, 3, 4)
    cb = c.reshape(bsz, h, n, Q_BLOCK).transpose(2, 0, 1, 3)
    qpos = jnp.arange(s).reshape(n, Q_BLOCK)
    kpos = jnp.arange(s)

    def block(args):
        qi, ci, pi = args
        logits = (jnp.einsum('bhqd,bhkd->bhqk', qi, kh) * scale
                  + ci[..., None] - c[:, :, None, :])
        logits = jnp.where(pi[:, None] >= kpos[None, :], logits, NEG_INF)
        p = jax.nn.softmax(logits, axis=-1)
        return jnp.einsum('bhqk,bhkd->bhqd', p, vh)

    out = lax.map(block, (qb, cb, qpos))
    return out.transpose(1, 0, 3, 2, 4).reshape(bsz, s, h, d)


def pool_mixer(u, w_pool, pool_scale):
    bsz, s, p = u.shape
    cs = jnp.cumsum(u, axis=1)
    t = jnp.arange(s, dtype=jnp.float32) + 1.0
    groups = []
    for gi, w in enumerate(POOL_WINDOWS):
        sl = slice(gi * POOL_GROUP_DIM, (gi + 1) * POOL_GROUP_DIM)
        csg = cs[..., sl]
        prev = jnp.pad(csg, ((0, 0), (w, 0), (0, 0)))[:, :s]
        mean = (csg - prev) / jnp.minimum(t, float(w))[None, :, None]
        groups.append(mean - u[..., sl])
    pooled = jnp.stack(groups, axis=2)
    mixed = jnp.einsum('bsgc,gcd->bsgd', pooled, w_pool.astype(jnp.float32)).reshape(bsz, s, p)
    return mixed * pool_scale.astype(jnp.float32)


def hybrid_layer(x, norm_g, w_in, ret_gn_g, fox_b_f, pool_w, pool_scale,
                 w_ret_branch, w_fox_branch, w_pool_branch, w_out):
    bsz, s, _ = x.shape
    h = rms_norm(x, norm_g)
    proj = jnp.matmul(h, w_in).astype(jnp.float32)
    sizes = [RET_WIDTH] * 4 + [FOX_WIDTH] * 4 + [N_FOX_HEADS] + [POOL_WIDTH] * 2 + [D_MODEL] * N_BRANCH
    cuts = np.cumsum(sizes)[:-1].tolist()
    (rq, rk, rv, rz, fq, fk, fv, fz, ff, pu, pz, ga, gb, gc) = jnp.split(proj, cuts, axis=-1)

    hs = (bsz, s, N_RET_HEADS, HEAD_DIM)
    y_ret = retention(rq.reshape(hs), rk.reshape(hs), rv.reshape(hs))
    y_ret = head_group_norm(y_ret, ret_gn_g).reshape(bsz, s, RET_WIDTH) * jax.nn.silu(rz)

    hs = (bsz, s, N_FOX_HEADS, HEAD_DIM)
    f_logit = ff + fox_b_f.astype(jnp.float32)
    y_fox = forgetting_attention(fq.reshape(hs), fk.reshape(hs), fv.reshape(hs), f_logit)
    y_fox = y_fox.reshape(bsz, s, FOX_WIDTH) * jax.nn.silu(fz)

    y_pool = pool_mixer(pu, pool_w, pool_scale) * jax.nn.silu(pz)

    merged = (jax.nn.sigmoid(ga) * jnp.matmul(y_ret, w_ret_branch.astype(jnp.float32))
              + jax.nn.sigmoid(gb) * jnp.matmul(y_fox, w_fox_branch.astype(jnp.float32))
              + jax.nn.sigmoid(gc) * jnp.matmul(y_pool, w_pool_branch.astype(jnp.float32)))
    out = jnp.matmul(merged, w_out.astype(jnp.float32))
    return x + out.astype(x.dtype)


def setup_inputs(seed: int = 0) -> dict:
    key = jax.random.key(seed)
    ks = jax.random.split(key, 13)
    f32 = jnp.float32
    x = jax.random.normal(ks[0], (BATCH, SEQ, D_MODEL), f32)
    norm_g = 1.0 + 0.02 * jax.random.normal(ks[1], (DEPTH, D_MODEL), f32)
    w_in = jax.random.normal(ks[2], (DEPTH, D_MODEL, C_IN), f32) * (D_MODEL ** -0.5)
    ret_gn_g = 1.0 + 0.02 * jax.random.normal(ks[3], (DEPTH, RET_WIDTH), f32)
    fox_b_f = 2.0 + 0.5 * jax.random.normal(ks[4], (DEPTH, N_FOX_HEADS), f32)
    pool_w = jax.random.normal(ks[5], (DEPTH, N_POOL_GROUPS, POOL_GROUP_DIM, POOL_GROUP_DIM), f32) * (POOL_GROUP_DIM ** -0.5)
    pool_scale = 1.0 + 0.02 * jax.random.normal(ks[6], (DEPTH, POOL_WIDTH), f32)
    w_ret_branch = jax.random.normal(ks[7], (DEPTH, RET_WIDTH, D_MODEL), f32) * (RET_WIDTH ** -0.5)
    w_fox_branch = jax.random.normal(ks[8], (DEPTH, FOX_WIDTH, D_MODEL), f32) * (FOX_WIDTH ** -0.5)
    w_pool_branch = jax.random.normal(ks[9], (DEPTH, POOL_WIDTH, D_MODEL), f32) * (POOL_WIDTH ** -0.5)
    w_out = jax.random.normal(ks[10], (DEPTH, D_MODEL, D_MODEL), f32) * (D_MODEL ** -0.5)
    final_g = 1.0 + 0.02 * jax.random.normal(ks[11], (D_MODEL,), f32)
    return {'x': x, 'norm_g': norm_g, 'w_in': w_in, 'ret_gn_g': ret_gn_g, 'fox_b_f': fox_b_f,
            'pool_w': pool_w, 'pool_scale': pool_scale, 'w_ret_branch': w_ret_branch,
            'w_fox_branch': w_fox_branch, 'w_pool_branch': w_pool_branch, 'w_out': w_out,
            'final_g': final_g}


def reference(x, norm_g, w_in, ret_gn_g, fox_b_f, pool_w, pool_scale,
              w_ret_branch, w_fox_branch, w_pool_branch, w_out, final_g):
    for layer in range(DEPTH):
        x = hybrid_layer(x, norm_g[layer], w_in[layer], ret_gn_g[layer], fox_b_f[layer],
                         pool_w[layer], pool_scale[layer], w_ret_branch[layer],
                         w_fox_branch[layer], w_pool_branch[layer], w_out[layer])
    return rms_norm(x, final_g)
```

```python
import functools

import jax
import jax.numpy as jnp
from jax import lax
from jax.experimental import pallas as pl
from jax.experimental.pallas import tpu as pltpu

D_MODEL = 2048
HEAD_DIM = 128
N_HEADS = 8
WIDTH = D_MODEL // 2
POOL_WINDOWS = (2, 4, 8, 16)
POOL_GROUP_DIM = WIDTH // len(POOL_WINDOWS)
CHUNK = 128
ROPE_THETA = 10000.0
EPS = 1e-6
NEG_INF = -1e30

LANES = 128
N_MAIN = 8 * WIDTH + 3 * D_MODEL
COL_RET = 0
COL_FOX = 4 * N_HEADS
COL_POOL_U = 8 * N_HEADS
COL_POOL_Z = 9 * N_HEADS
COL_GATES = 10 * N_HEADS

BF16 = jnp.bfloat16
F32 = jnp.float32

VMEM_LIMIT = 56 * 1024 * 1024


def _params(sem, vmem=VMEM_LIMIT):
    return pltpu.CompilerParams(dimension_semantics=sem, vmem_limit_bytes=vmem)


def _inproj_kernel(x_ref, g_ref, w_ref, wf_ref, o_ref, f_ref, h_scr):
    @pl.when(pl.program_id(1) == 0)
    def _():
        x = x_ref[...]
        y = x * lax.rsqrt(jnp.mean(x * x, axis=-1, keepdims=True) + EPS)
        h = (y * g_ref[...]).astype(BF16)
        h_scr[...] = h
        f_ref[...] = jnp.dot(h, wf_ref[...], preferred_element_type=F32)

    o_ref[...] = jnp.dot(h_scr[...], w_ref[...], preferred_element_type=F32)


def _inproj(x2, g, w_main, w_f, tm=1024, tn=1024):
    m, d = x2.shape
    n = w_main.shape[1]
    return pl.pallas_call(
        _inproj_kernel,
        grid=(m // tm, n // tn),
        in_specs=[
            pl.BlockSpec((tm, d), lambda i, j: (i, 0)),
            pl.BlockSpec((1, d), lambda i, j: (0, 0)),
            pl.BlockSpec((d, tn), lambda i, j: (0, j)),
            pl.BlockSpec((d, LANES), lambda i, j: (0, 0)),
        ],
        out_specs=[
            pl.BlockSpec((tm, tn), lambda i, j: (i, j)),
            pl.BlockSpec((tm, LANES), lambda i, j: (i, 0)),
        ],
        out_shape=[
            jax.ShapeDtypeStruct((m, n), F32),
            jax.ShapeDtypeStruct((m, LANES), F32),
        ],
        scratch_shapes=[pltpu.VMEM((tm, d), BF16)],
        compiler_params=_params(("parallel", "arbitrary")),
        name="inproj",
    )(x2, g, w_main, w_f)


def _rotate(t, cos, sin_signed):
    return t * cos + pltpu.roll(t, HEAD_DIM // 2, 1) * sin_signed


def _retention_kernel(q_ref, k_ref, v_ref, z_ref, cos_ref, sin_ref, decay_ref,
                      xi_ref, zeta_ref, gch_ref, gn_ref, o_ref, state_scr):
    state_scr[...] = jnp.zeros_like(state_scr)
    decay = decay_ref[0]
    xi = xi_ref[0]
    zeta = zeta_ref[0]
    g_chunk = gch_ref[0, 0:1, :]
    gn = gn_ref[0]
    n_chunks = q_ref.shape[0] // CHUNK

    def step(c, carry):
        rows = pl.ds(pl.multiple_of(c * CHUNK, CHUNK), CHUNK)
        cos = cos_ref[rows, :]
        sin = sin_ref[rows, :]
        q = _rotate(q_ref[rows, :], cos, sin)
        k = _rotate(k_ref[rows, :], cos, sin) * (HEAD_DIM ** -0.5)
        v = v_ref[rows, :].astype(BF16)
        qb = q.astype(BF16)
        state = state_scr[...]
        inner = lax.dot_general(qb, k.astype(BF16), (((1,), (1,)), ((), ())),
                                preferred_element_type=F32) * decay
        out = (jnp.dot(inner.astype(BF16), v, preferred_element_type=F32)
               + jnp.dot(qb, state.astype(BF16), preferred_element_type=F32) * xi)
        kz = (k * zeta).astype(BF16)
        state_scr[...] = g_chunk * state + lax.dot_general(
            kz, v, (((0,), (0,)), ((), ())), preferred_element_type=F32)
        mu = jnp.mean(out, axis=-1, keepdims=True)
        dev = out - mu
        var = jnp.mean(dev * dev, axis=-1, keepdims=True)
        yn = dev * lax.rsqrt(var + EPS) * gn
        z = z_ref[rows, :]
        o_ref[rows, :] = (yn * (z * jax.nn.sigmoid(z))).astype(o_ref.dtype)
        return carry

    lax.fori_loop(0, n_chunks, step, 0)


def _retention(proj, tables, gn_g, batch, seq):
    cos, sin, decay, xi, zeta, gch = tables
    m = proj.shape[0]
    head_block = lambda off: pl.BlockSpec((seq, HEAD_DIM), lambda b, h: (b, off + h))
    per_head = lambda shape: pl.BlockSpec((1,) + shape, lambda b, h: (h, 0, 0))
    return pl.pallas_call(
        _retention_kernel,
        grid=(batch, N_HEADS),
        in_specs=[
            head_block(COL_RET), head_block(COL_RET + N_HEADS),
            head_block(COL_RET + 2 * N_HEADS), head_block(COL_RET + 3 * N_HEADS),
            pl.BlockSpec((seq, HEAD_DIM), lambda b, h: (0, 0)),
            pl.BlockSpec((seq, HEAD_DIM), lambda b, h: (0, 0)),
            per_head((CHUNK, CHUNK)), per_head((CHUNK, HEAD_DIM)),
            per_head((CHUNK, HEAD_DIM)), per_head((8, HEAD_DIM)),
            per_head((1, HEAD_DIM)),
        ],
        out_specs=pl.BlockSpec((seq, HEAD_DIM), lambda b, h: (b, h)),
        out_shape=jax.ShapeDtypeStruct((m, WIDTH), BF16),
        scratch_shapes=[pltpu.VMEM((HEAD_DIM, HEAD_DIM), F32)],
        compiler_params=_params(("parallel", "arbitrary")),
        name="retention",
    )(proj, proj, proj, proj, cos, sin, decay, xi, zeta, gch,
      gn_g.reshape(N_HEADS, 1, HEAD_DIM))


def _retention_tables(seq):
    half = HEAD_DIM // 2
    inv = ROPE_THETA ** (-jnp.arange(half, dtype=F32) / half)
    ang = jnp.arange(seq, dtype=F32)[:, None] * inv[None, :]
    cos = jnp.cos(ang)
    sin = jnp.sin(ang)
    cos2 = jnp.concatenate([cos, cos], axis=-1)
    sin2 = jnp.concatenate([-sin, sin], axis=-1)
    log_g = jnp.log1p(-jnp.exp2(-5.0 - jnp.arange(N_HEADS, dtype=F32)))
    idx = jnp.arange(CHUNK, dtype=F32)
    diff = idx[:, None] - idx[None, :]
    decay = jnp.where(diff >= 0, jnp.exp(log_g[:, None, None] * jnp.maximum(diff, 0.0)), 0.0)
    xi = jnp.exp(log_g[:, None] * (idx + 1.0))
    zeta = jnp.exp(log_g[:, None] * (CHUNK - 1.0 - idx))
    g_chunk = jnp.exp(log_g * CHUNK)
    bcast = lambda t: jnp.broadcast_to(t[:, :, None], (N_HEADS, CHUNK, HEAD_DIM))
    gch = jnp.broadcast_to(g_chunk[:, None, None], (N_HEADS, 8, HEAD_DIM))
    return cos2, sin2, decay, bcast(xi), bcast(zeta), gch


def _split3(x):
    hi = x.astype(BF16)
    r1 = x - hi.astype(F32)
    mid = r1.astype(BF16)
    lo = (r1 - mid.astype(F32)).astype(BF16)
    return hi, mid, lo


def _forget_cumsum_kernel(f_ref, b_ref, col_ref, row_ref):
    n_blocks = f_ref.shape[0] // CHUNK
    r = lax.broadcasted_iota(jnp.int32, (CHUNK, CHUNK), 0)
    c = lax.broadcasted_iota(jnp.int32, (CHUNK, CHUNK), 1)
    tri = jnp.where(r >= c, 1.0, 0.0).astype(BF16)
    carry = jnp.zeros((1, LANES), F32)
    for blk in range(n_blocks):
        rows = pl.ds(blk * CHUNK, CHUNK)
        x = f_ref[rows, :] + b_ref[...]
        ls = jnp.minimum(x, 0.0) - jnp.log1p(jnp.exp(-jnp.abs(x)))
        hi, mid, lo = _split3(ls)
        cs = (jnp.dot(tri, hi, preferred_element_type=F32)
              + jnp.dot(tri, mid, preferred_element_type=F32)
              + jnp.dot(tri, lo, preferred_element_type=F32)) + carry
        col_ref[rows, :] = cs
        row_ref[0, :, rows] = cs.T
        carry = cs[CHUNK - 1:CHUNK, :]


def _forget_cumsum(f_logits, bias, batch, seq):
    m = f_logits.shape[0]
    return pl.pallas_call(
        _forget_cumsum_kernel,
        grid=(batch,),
        in_specs=[
            pl.BlockSpec((seq, LANES), lambda b: (b, 0)),
            pl.BlockSpec((1, LANES), lambda b: (0, 0)),
        ],
        out_specs=[
            pl.BlockSpec((seq, LANES), lambda b: (b, 0)),
            pl.BlockSpec((1, LANES, seq), lambda b: (b, 0, 0)),
        ],
        out_shape=[
            jax.ShapeDtypeStruct((m, LANES), F32),
            jax.ShapeDtypeStruct((batch, LANES, seq), F32),
        ],
        compiler_params=_params(("parallel",)),
        name="forget_cumsum",
    )(f_logits, bias)


def _fox_kernel(q_ref, k_ref, v_ref, z_ref, ccol_ref, crow_ref, o_ref, k_scr, v_scr, *, tq):
    h = pl.program_id(1)
    qi = pl.program_id(2)
    scale = HEAD_DIM ** -0.5
    sub = tq // CHUNK

    @pl.when(qi == 0)
    def _():
        k_scr[...] = k_ref[...].astype(BF16)
        v_scr[...] = v_ref[...].astype(BF16)

    q = q_ref[...].astype(BF16)
    lane = lax.broadcasted_iota(jnp.int32, (tq, LANES), 1)
    cq = jnp.sum(jnp.where(lane == h, ccol_ref[...], 0.0), axis=-1, keepdims=True)

    def scores(j):
        rows = pl.ds(pl.multiple_of(j * CHUNK, CHUNK), CHUNK)
        s = lax.dot_general(q, k_scr[rows, :], (((1,), (1,)), ((), ())),
                            preferred_element_type=F32)
        return s * scale + (cq - crow_ref[0, j]), rows

    def update(s, rows, carry):
        m, l, acc = carry
        m_new = jnp.maximum(m, jnp.max(s, axis=-1, keepdims=True))
        alpha = jnp.exp(m - m_new)
        p = jnp.exp(s - m_new)
        l = alpha * l + jnp.sum(p, axis=-1, keepdims=True)
        acc = alpha * acc + jnp.dot(p.astype(BF16), v_scr[rows, :], preferred_element_type=F32)
        return m_new, l, acc

    def full_block(j, carry):
        s, rows = scores(j)
        return update(s, rows, carry)

    carry = (jnp.full((tq, 1), NEG_INF, F32), jnp.zeros((tq, 1), F32),
             jnp.zeros((tq, HEAD_DIM), F32))
    carry = lax.fori_loop(0, qi * sub, full_block, carry)
    row = lax.broadcasted_iota(jnp.int32, (tq, CHUNK), 0)
    col = lax.broadcasted_iota(jnp.int32, (tq, CHUNK), 1)
    for d in range(sub):
        s, rows = scores(qi * sub + d)
        s = jnp.where(row >= col + d * CHUNK, s, NEG_INF)
        carry = update(s, rows, carry)
    _, l, acc = carry
    z = z_ref[...]
    o_ref[...] = ((acc / l) * (z * jax.nn.sigmoid(z))).astype(o_ref.dtype)


def _fox(proj, c_col, c_row, batch, seq, tq=512):
    m = proj.shape[0]
    nq = seq // tq
    nk = seq // CHUNK
    q_block = lambda off: pl.BlockSpec((tq, HEAD_DIM), lambda b, h, i: (b * nq + i, off + h))
    kv_block = lambda off: pl.BlockSpec((seq, HEAD_DIM), lambda b, h, i: (b, off + h))
    return pl.pallas_call(
        functools.partial(_fox_kernel, tq=tq),
        grid=(batch, N_HEADS, nq),
        in_specs=[
            q_block(COL_FOX), kv_block(COL_FOX + N_HEADS), kv_block(COL_FOX + 2 * N_HEADS),
            q_block(COL_FOX + 3 * N_HEADS),
            pl.BlockSpec((tq, LANES), lambda b, h, i: (b * nq + i, 0)),
            pl.BlockSpec((1, nk, 1, CHUNK), lambda b, h, i: (b * N_HEADS + h, 0, 0, 0)),
        ],
        out_specs=pl.BlockSpec((tq, HEAD_DIM), lambda b, h, i: (b * nq + i, h)),
        out_shape=jax.ShapeDtypeStruct((m, WIDTH), BF16),
        scratch_shapes=[pltpu.VMEM((seq, HEAD_DIM), BF16), pltpu.VMEM((seq, HEAD_DIM), BF16)],
        compiler_params=_params(("parallel", "arbitrary", "arbitrary")),
        name="fox_attention",
    )(proj, proj, proj, proj, c_col, c_row)


def _pool_kernel(u_ref, z_ref, w_ref, s_ref, o_ref):
    seq = u_ref.shape[0]
    g = pl.program_id(1)
    u = u_ref[...]
    t = lax.broadcasted_iota(jnp.int32, (seq, POOL_GROUP_DIM), 0)

    def shifted(x, k):
        return jnp.where(t >= k, pltpu.roll(x, k, 0), 0.0)

    s2 = u + shifted(u, 1)
    s4 = s2 + shifted(s2, 2)
    s8 = s4 + shifted(s4, 4)
    s16 = s8 + shifted(s8, 8)
    tf = (t + 1).astype(F32)
    pooled = jnp.zeros_like(u)
    for gi, (w, sw) in enumerate(zip(POOL_WINDOWS, (s2, s4, s8, s16))):
        mean = sw / jnp.minimum(tf, float(w))
        pooled = jnp.where(g == gi, mean, pooled)
    pooled = pooled - u
    mixed = jnp.dot(pooled.astype(BF16), w_ref[0], preferred_element_type=F32) * s_ref[0]
    z = z_ref[...]
    o_ref[...] = (mixed * (z * jax.nn.sigmoid(z))).astype(o_ref.dtype)


def _pool(proj, pool_w, pool_scale, batch, seq):
    m = proj.shape[0]
    n_groups = len(POOL_WINDOWS)
    blocks_per_group = POOL_GROUP_DIM // LANES
    grp = lambda off: pl.BlockSpec(
        (seq, POOL_GROUP_DIM), lambda b, g: (b, off // blocks_per_group + g))
    return pl.pallas_call(
        _pool_kernel,
        grid=(batch, n_groups),
        in_specs=[
            grp(COL_POOL_U), grp(COL_POOL_Z),
            pl.BlockSpec((1, POOL_GROUP_DIM, POOL_GROUP_DIM), lambda b, g: (g, 0, 0)),
            pl.BlockSpec((1, 1, POOL_GROUP_DIM), lambda b, g: (g, 0, 0)),
        ],
        out_specs=pl.BlockSpec((seq, POOL_GROUP_DIM), lambda b, g: (b, g)),
        out_shape=jax.ShapeDtypeStruct((m, WIDTH), BF16),
        compiler_params=_params(("parallel", "arbitrary")),
        name="pool_mixer",
    )(proj, proj, pool_w.astype(BF16), pool_scale.reshape(n_groups, 1, POOL_GROUP_DIM))


def _merge_kernel(x_ref, yr_ref, yf_ref, yp_ref, ga_ref, gb_ref, gc_ref,
                  wr_ref, wf_ref, wp_ref, wo_ref, fg_ref, o_ref, *, final_norm):
    merged = (jax.nn.sigmoid(ga_ref[...]) * jnp.dot(yr_ref[...], wr_ref[...], preferred_element_type=F32)
              + jax.nn.sigmoid(gb_ref[...]) * jnp.dot(yf_ref[...], wf_ref[...], preferred_element_type=F32)
              + jax.nn.sigmoid(gc_ref[...]) * jnp.dot(yp_ref[...], wp_ref[...], preferred_element_type=F32))
    x = x_ref[...] + jnp.dot(merged.astype(BF16), wo_ref[...], preferred_element_type=F32)
    if final_norm:
        x = x * lax.rsqrt(jnp.mean(x * x, axis=-1, keepdims=True) + EPS) * fg_ref[...]
    o_ref[...] = x


def _merge(x2, y_ret, y_fox, y_pool, proj, w_r, w_f, w_p, w_o, final_g, final_norm, tm=256):
    m, d = x2.shape
    gate_blocks = d // LANES
    row = lambda width: pl.BlockSpec((tm, width), lambda i: (i, 0))
    gate = lambda k: pl.BlockSpec((tm, d), lambda i: (i, COL_GATES // gate_blocks + k))
    const = lambda shape: pl.BlockSpec(shape, lambda i: (0, 0), pipeline_mode=pl.Buffered(1))
    return pl.pallas_call(
        functools.partial(_merge_kernel, final_norm=final_norm),
        grid=(m // tm,),
        in_specs=[
            row(d), row(WIDTH), row(WIDTH), row(WIDTH), gate(0), gate(1), gate(2),
            const((WIDTH, d)), const((WIDTH, d)), const((WIDTH, d)), const((d, d)),
            const((1, d)),
        ],
        out_specs=row(d),
        out_shape=jax.ShapeDtypeStruct((m, d), F32),
        compiler_params=_params(("parallel",)),
        name="merge_out",
    )(x2, y_ret, y_fox, y_pool, proj, proj, proj, w_r, w_f, w_p, w_o, final_g)


def kernel(x, norm_g, w_in, ret_gn_g, fox_b_f, pool_w, pool_scale,
           w_ret_branch, w_fox_branch, w_pool_branch, w_out, final_g):
    batch, seq, d = x.shape
    depth = w_in.shape[0]
    n_fox_cols = 8 * WIDTH
    tables = _retention_tables(seq)
    x2 = x.reshape(batch * seq, d)
    for layer in range(depth):
        w = w_in[layer]
        w_main = jnp.concatenate(
            [w[:, :n_fox_cols], w[:, n_fox_cols + N_HEADS:]], axis=1).astype(BF16)
        w_forget = jnp.pad(w[:, n_fox_cols:n_fox_cols + N_HEADS],
                           ((0, 0), (0, LANES - N_HEADS))).astype(BF16)
        bias = jnp.pad(fox_b_f[layer], (0, LANES - N_HEADS)).reshape(1, LANES)

        proj, f_logits = _inproj(x2, norm_g[layer].reshape(1, d), w_main, w_forget)
        y_ret = _retention(proj, tables, ret_gn_g[layer], batch, seq)
        c_col, c_row_t = _forget_cumsum(f_logits, bias, batch, seq)
        c_row = c_row_t[:, :N_HEADS, :].reshape(batch * N_HEADS, seq // CHUNK, 1, CHUNK)
        y_fox = _fox(proj, c_col, c_row, batch, seq)
        y_pool = _pool(proj, pool_w[layer], pool_scale[layer], batch, seq)
        x2 = _merge(x2, y_ret, y_fox, y_pool, proj,
                    w_ret_branch[layer].astype(BF16), w_fox_branch[layer].astype(BF16),
                    w_pool_branch[layer].astype(BF16), w_out[layer].astype(BF16),
                    final_g.reshape(1, d), final_norm=(layer == depth - 1))
    return x2.reshape(batch, seq, d)
```

```python
import functools

import jax
import jax.numpy as jnp
from jax import lax
from jax.experimental import pallas as pl
from jax.experimental.pallas import tpu as pltpu

D_MODEL = 2048
HEAD_DIM = 128
N_HEADS = 8
WIDTH = D_MODEL // 2
POOL_WINDOWS = (2, 4, 8, 16)
POOL_GROUP_DIM = WIDTH // len(POOL_WINDOWS)
CHUNK = 128
ROPE_THETA = 10000.0
EPS = 1e-6
NEG_INF = -1e30
LOG2E = 1.4426950408889634
FOX_BLOCK = 256

LANES = 128
N_MAIN = 8 * WIDTH + 3 * D_MODEL
COL_RET = 0
COL_FOX = 4 * N_HEADS
COL_POOL_U = 8 * N_HEADS
COL_POOL_Z = 9 * N_HEADS
COL_GATES = 10 * N_HEADS

BF16 = jnp.bfloat16
F32 = jnp.float32

VMEM_LIMIT = 56 * 1024 * 1024


def _params(sem, vmem=VMEM_LIMIT):
    return pltpu.CompilerParams(dimension_semantics=sem, vmem_limit_bytes=vmem)


def _inproj_kernel(x_ref, g_ref, w_ref, wf_ref, o_ref, f_ref, h_scr):
    @pl.when(pl.program_id(1) == 0)
    def _():
        x = x_ref[...]
        y = x * lax.rsqrt(jnp.mean(x * x, axis=-1, keepdims=True) + EPS)
        h = (y * g_ref[...]).astype(BF16)
        h_scr[...] = h
        f_ref[...] = jnp.dot(h, wf_ref[...], preferred_element_type=F32)

    o_ref[...] = jnp.dot(h_scr[...], w_ref[...], preferred_element_type=F32)


def _inproj(x2, g, w_main, w_f, tm=1024, tn=1024):
    m, d = x2.shape
    n = w_main.shape[1]
    return pl.pallas_call(
        _inproj_kernel,
        grid=(m // tm, n // tn),
        in_specs=[
            pl.BlockSpec((tm, d), lambda i, j: (i, 0)),
            pl.BlockSpec((1, d), lambda i, j: (0, 0)),
            pl.BlockSpec((d, tn), lambda i, j: (0, j)),
            pl.BlockSpec((d, LANES), lambda i, j: (0, 0)),
        ],
        out_specs=[
            pl.BlockSpec((tm, tn), lambda i, j: (i, j)),
            pl.BlockSpec((tm, LANES), lambda i, j: (i, 0)),
        ],
        out_shape=[
            jax.ShapeDtypeStruct((m, n), F32),
            jax.ShapeDtypeStruct((m, LANES), F32),
        ],
        scratch_shapes=[pltpu.VMEM((tm, d), BF16)],
        compiler_params=_params(("parallel", "arbitrary")),
        name="inproj",
    )(x2, g, w_main, w_f)


def _rotate(t, cos, sin_signed):
    return t * cos + pltpu.roll(t, HEAD_DIM // 2, 1) * sin_signed


def _retention_kernel(q_ref, k_ref, v_ref, z_ref, cos_ref, sin_ref, decay_ref,
                      xi_ref, zeta_ref, gch_ref, gn_ref, o_ref, k_scr):
    decay = decay_ref[0]
    xi = xi_ref[0]
    zeta = zeta_ref[0]
    g_chunk = gch_ref[0, 0:1, :]
    gn = gn_ref[0]
    n_chunks = q_ref.shape[0] // CHUNK
    chunk_rows = [pl.ds(c * CHUNK, CHUNK) for c in range(n_chunks)]

    increments = []
    for rows in chunk_rows:
        k = _rotate(k_ref[rows, :], cos_ref[rows, :], sin_ref[rows, :]) * (HEAD_DIM ** -0.5)
        k_scr[rows, :] = k.astype(BF16)
        increments.append(lax.dot_general(
            (k * zeta).astype(BF16), v_ref[rows, :].astype(BF16),
            (((0,), (0,)), ((), ())), preferred_element_type=F32))

    state = jnp.zeros((HEAD_DIM, HEAD_DIM), F32)
    states = []
    for inc in increments:
        states.append(state.astype(BF16))
        state = g_chunk * state + inc

    for rows, state_b in zip(chunk_rows, states):
        qb = _rotate(q_ref[rows, :], cos_ref[rows, :], sin_ref[rows, :]).astype(BF16)
        inner = lax.dot_general(qb, k_scr[rows, :], (((1,), (1,)), ((), ())),
                                preferred_element_type=F32) * decay
        out = (jnp.dot(inner.astype(BF16), v_ref[rows, :].astype(BF16), preferred_element_type=F32)
               + jnp.dot(qb, state_b, preferred_element_type=F32) * xi)
        mu = jnp.mean(out, axis=-1, keepdims=True)
        dev = out - mu
        var = jnp.mean(dev * dev, axis=-1, keepdims=True)
        yn = dev * lax.rsqrt(var + EPS) * gn
        z = z_ref[rows, :]
        o_ref[rows, :] = (yn * (z * jax.nn.sigmoid(z))).astype(o_ref.dtype)


def _retention(proj, tables, gn_g, batch, seq):
    cos, sin, decay, xi, zeta, gch = tables
    m = proj.shape[0]
    head_block = lambda off: pl.BlockSpec((seq, HEAD_DIM), lambda b, h: (b, off + h))
    per_head = lambda shape: pl.BlockSpec((1,) + shape, lambda b, h: (h, 0, 0))
    return pl.pallas_call(
        _retention_kernel,
        grid=(batch, N_HEADS),
        in_specs=[
            head_block(COL_RET), head_block(COL_RET + N_HEADS),
            head_block(COL_RET + 2 * N_HEADS), head_block(COL_RET + 3 * N_HEADS),
            pl.BlockSpec((seq, HEAD_DIM), lambda b, h: (0, 0)),
            pl.BlockSpec((seq, HEAD_DIM), lambda b, h: (0, 0)),
            per_head((CHUNK, CHUNK)), per_head((CHUNK, HEAD_DIM)),
            per_head((CHUNK, HEAD_DIM)), per_head((8, HEAD_DIM)),
            per_head((1, HEAD_DIM)),
        ],
        out_specs=pl.BlockSpec((seq, HEAD_DIM), lambda b, h: (b, h)),
        out_shape=jax.ShapeDtypeStruct((m, WIDTH), BF16),
        scratch_shapes=[pltpu.VMEM((seq, HEAD_DIM), BF16)],
        compiler_params=_params(("parallel", "arbitrary")),
        name="retention",
    )(proj, proj, proj, proj, cos, sin, decay, xi, zeta, gch,
      gn_g.reshape(N_HEADS, 1, HEAD_DIM))


def _retention_tables(seq):
    half = HEAD_DIM // 2
    inv = ROPE_THETA ** (-jnp.arange(half, dtype=F32) / half)
    ang = jnp.arange(seq, dtype=F32)[:, None] * inv[None, :]
    cos = jnp.cos(ang)
    sin = jnp.sin(ang)
    cos2 = jnp.concatenate([cos, cos], axis=-1)
    sin2 = jnp.concatenate([-sin, sin], axis=-1)
    log_g = jnp.log1p(-jnp.exp2(-5.0 - jnp.arange(N_HEADS, dtype=F32)))
    idx = jnp.arange(CHUNK, dtype=F32)
    diff = idx[:, None] - idx[None, :]
    decay = jnp.where(diff >= 0, jnp.exp(log_g[:, None, None] * jnp.maximum(diff, 0.0)), 0.0)
    xi = jnp.exp(log_g[:, None] * (idx + 1.0))
    zeta = jnp.exp(log_g[:, None] * (CHUNK - 1.0 - idx))
    g_chunk = jnp.exp(log_g * CHUNK)
    bcast = lambda t: jnp.broadcast_to(t[:, :, None], (N_HEADS, CHUNK, HEAD_DIM))
    gch = jnp.broadcast_to(g_chunk[:, None, None], (N_HEADS, 8, HEAD_DIM))
    return cos2, sin2, decay, bcast(xi), bcast(zeta), gch


def _split3(x):
    hi = x.astype(BF16)
    r1 = x - hi.astype(F32)
    mid = r1.astype(BF16)
    lo = (r1 - mid.astype(F32)).astype(BF16)
    return hi, mid, lo


def _forget_cumsum_kernel(f_ref, b_ref, col_ref, row_ref):
    n_blocks = f_ref.shape[0] // CHUNK
    r = lax.broadcasted_iota(jnp.int32, (CHUNK, CHUNK), 0)
    c = lax.broadcasted_iota(jnp.int32, (CHUNK, CHUNK), 1)
    tri = jnp.where(r >= c, 1.0, 0.0).astype(BF16)
    carry = jnp.zeros((1, LANES), F32)
    for blk in range(n_blocks):
        rows = pl.ds(blk * CHUNK, CHUNK)
        x = f_ref[rows, :] + b_ref[...]
        ls = jnp.minimum(x, 0.0) - jnp.log1p(jnp.exp(-jnp.abs(x)))
        hi, mid, lo = _split3(ls)
        cs = (jnp.dot(tri, hi, preferred_element_type=F32)
              + jnp.dot(tri, mid, preferred_element_type=F32)
              + jnp.dot(tri, lo, preferred_element_type=F32)) + carry
        col_ref[rows, :] = cs
        row_ref[0, :, rows] = cs.T
        carry = cs[CHUNK - 1:CHUNK, :]


def _forget_cumsum(f_logits, bias, batch, seq):
    m = f_logits.shape[0]
    return pl.pallas_call(
        _forget_cumsum_kernel,
        grid=(batch,),
        in_specs=[
            pl.BlockSpec((seq, LANES), lambda b: (b, 0)),
            pl.BlockSpec((1, LANES), lambda b: (0, 0)),
        ],
        out_specs=[
            pl.BlockSpec((seq, LANES), lambda b: (b, 0)),
            pl.BlockSpec((1, LANES, seq), lambda b: (b, 0, 0)),
        ],
        out_shape=[
            jax.ShapeDtypeStruct((m, LANES), F32),
            jax.ShapeDtypeStruct((batch, LANES, seq), F32),
        ],
        compiler_params=_params(("parallel",)),
        name="forget_cumsum",
    )(f_logits, bias)


def _lane_groups(x):
    return [x[:, g * LANES:(g + 1) * LANES] for g in range(x.shape[1] // LANES)]


def _fox_kernel(q_ref, k_ref, v_ref, z_ref, ccol_ref, crow_ref, o_ref,
                m_scr, l_scr, acc_scr, *, tq):
    h = pl.program_id(1)
    seq = q_ref.shape[0]
    c1 = (HEAD_DIM ** -0.5) * LOG2E

    qb = q_ref[...].astype(BF16)
    lane = lax.broadcasted_iota(jnp.int32, (seq, LANES), 1)
    cq2 = jnp.sum(jnp.where(lane == h, ccol_ref[...], 0.0), axis=-1, keepdims=True) * LOG2E
    causal = (lax.broadcasted_iota(jnp.int32, (tq, tq), 0)
              >= lax.broadcasted_iota(jnp.int32, (tq, tq), 1))

    for j in range(seq // tq):
        lo = j * tq
        live = slice(lo, seq)
        kj = k_ref[lo:lo + tq, :].astype(BF16)
        vj = v_ref[lo:lo + tq, :].astype(BF16)
        s = lax.dot_general(qb[live], kj, (((1,), (1,)), ((), ())), preferred_element_type=F32)
        a = s * c1 - crow_ref[0, j] * LOG2E
        diag = jnp.where(causal, a[:tq], NEG_INF)
        a = diag if lo + tq == seq else jnp.concatenate([diag, a[tq:]], axis=0)
        groups = _lane_groups(a)
        m_blk = jnp.max(functools.reduce(jnp.maximum, groups), axis=-1, keepdims=True) + cq2[live]
        if j == 0:
            m_new = jnp.broadcast_to(m_blk, (seq, LANES))
        else:
            m_prev = m_scr[live, :]
            m_new = jnp.maximum(m_prev, m_blk)
            alpha = jnp.exp2(m_prev - m_new)
        shift = m_new - cq2[live]
        ps = [jnp.exp2(g - shift) for g in groups]
        p = jnp.concatenate([x.astype(BF16) for x in ps], axis=-1)
        p_sum = functools.reduce(jnp.add, ps)
        pv = jnp.dot(p, vj, preferred_element_type=F32)
        m_scr[live, :] = m_new
        if j == 0:
            l_scr[...] = p_sum
            acc_scr[...] = pv
        else:
            l_scr[live, :] = alpha * l_scr[live, :] + p_sum
            acc_scr[live, :] = alpha * acc_scr[live, :] + pv

    l = jnp.sum(l_scr[...], axis=-1, keepdims=True)
    z = z_ref[...]
    o_ref[...] = ((acc_scr[...] / l) * (z * jax.nn.sigmoid(z))).astype(o_ref.dtype)


def _fox(proj, c_col, c_row, batch, seq, tq):
    m = proj.shape[0]
    head_block = lambda off: pl.BlockSpec((seq, HEAD_DIM), lambda b, h: (b, off + h))
    stat = pltpu.VMEM((seq, LANES), F32)
    return pl.pallas_call(
        functools.partial(_fox_kernel, tq=tq),
        grid=(batch, N_HEADS),
        in_specs=[
            head_block(COL_FOX), head_block(COL_FOX + N_HEADS),
            head_block(COL_FOX + 2 * N_HEADS), head_block(COL_FOX + 3 * N_HEADS),
            pl.BlockSpec((seq, LANES), lambda b, h: (b, 0)),
            pl.BlockSpec((1, seq // tq, 1, tq), lambda b, h: (b * N_HEADS + h, 0, 0, 0)),
        ],
        out_specs=pl.BlockSpec((seq, HEAD_DIM), lambda b, h: (b, h)),
        out_shape=jax.ShapeDtypeStruct((m, WIDTH), BF16),
        scratch_shapes=[stat, stat, stat],
        compiler_params=_params(("parallel", "arbitrary")),
        name="fox_attention",
    )(proj, proj, proj, proj, c_col, c_row)


def _pool_kernel(u_ref, z_ref, w_ref, s_ref, o_ref):
    seq = u_ref.shape[0]
    g = pl.program_id(1)
    u = u_ref[...]
    t = lax.broadcasted_iota(jnp.int32, (seq, POOL_GROUP_DIM), 0)

    def shifted(x, k):
        return jnp.where(t >= k, pltpu.roll(x, k, 0), 0.0)

    s2 = u + shifted(u, 1)
    s4 = s2 + shifted(s2, 2)
    s8 = s4 + shifted(s4, 4)
    s16 = s8 + shifted(s8, 8)
    tf = (t + 1).astype(F32)
    pooled = jnp.zeros_like(u)
    for gi, (w, sw) in enumerate(zip(POOL_WINDOWS, (s2, s4, s8, s16))):
        mean = sw / jnp.minimum(tf, float(w))
        pooled = jnp.where(g == gi, mean, pooled)
    pooled = pooled - u
    mixed = jnp.dot(pooled.astype(BF16), w_ref[0], preferred_element_type=F32) * s_ref[0]
    z = z_ref[...]
    o_ref[...] = (mixed * (z * jax.nn.sigmoid(z))).astype(o_ref.dtype)


def _pool(proj, pool_w, pool_scale, batch, seq):
    m = proj.shape[0]
    n_groups = len(POOL_WINDOWS)
    blocks_per_group = POOL_GROUP_DIM // LANES
    grp = lambda off: pl.BlockSpec(
        (seq, POOL_GROUP_DIM), lambda b, g: (b, off // blocks_per_group + g))
    return pl.pallas_call(
        _pool_kernel,
        grid=(batch, n_groups),
        in_specs=[
            grp(COL_POOL_U), grp(COL_POOL_Z),
            pl.BlockSpec((1, POOL_GROUP_DIM, POOL_GROUP_DIM), lambda b, g: (g, 0, 0)),
            pl.BlockSpec((1, 1, POOL_GROUP_DIM), lambda b, g: (g, 0, 0)),
        ],
        out_specs=pl.BlockSpec((seq, POOL_GROUP_DIM), lambda b, g: (b, g)),
        out_shape=jax.ShapeDtypeStruct((m, WIDTH), BF16),
        compiler_params=_params(("parallel", "arbitrary")),
        name="pool_mixer",
    )(proj, proj, pool_w.astype(BF16), pool_scale.reshape(n_groups, 1, POOL_GROUP_DIM))


def _merge_kernel(x_ref, yr_ref, yf_ref, yp_ref, ga_ref, gb_ref, gc_ref,
                  wr_ref, wf_ref, wp_ref, wo_ref, fg_ref, o_ref, *, final_norm):
    merged = (jax.nn.sigmoid(ga_ref[...]) * jnp.dot(yr_ref[...], wr_ref[...], preferred_element_type=F32)
              + jax.nn.sigmoid(gb_ref[...]) * jnp.dot(yf_ref[...], wf_ref[...], preferred_element_type=F32)
              + jax.nn.sigmoid(gc_ref[...]) * jnp.dot(yp_ref[...], wp_ref[...], preferred_element_type=F32))
    x = x_ref[...] + jnp.dot(merged.astype(BF16), wo_ref[...], preferred_element_type=F32)
    if final_norm:
        x = x * lax.rsqrt(jnp.mean(x * x, axis=-1, keepdims=True) + EPS) * fg_ref[...]
    o_ref[...] = x


def _merge(x2, y_ret, y_fox, y_pool, proj, w_r, w_f, w_p, w_o, final_g, final_norm, tm=256):
    m, d = x2.shape
    gate_blocks = d // LANES
    row = lambda width: pl.BlockSpec((tm, width), lambda i: (i, 0))
    gate = lambda k: pl.BlockSpec((tm, d), lambda i: (i, COL_GATES // gate_blocks + k))
    const = lambda shape: pl.BlockSpec(shape, lambda i: (0, 0), pipeline_mode=pl.Buffered(1))
    return pl.pallas_call(
        functools.partial(_merge_kernel, final_norm=final_norm),
        grid=(m // tm,),
        in_specs=[
            row(d), row(WIDTH), row(WIDTH), row(WIDTH), gate(0), gate(1), gate(2),
            const((WIDTH, d)), const((WIDTH, d)), const((WIDTH, d)), const((d, d)),
            const((1, d)),
        ],
        out_specs=row(d),
        out_shape=jax.ShapeDtypeStruct((m, d), F32),
        compiler_params=_params(("parallel",)),
        name="merge_out",
    )(x2, y_ret, y_fox, y_pool, proj, proj, proj, w_r, w_f, w_p, w_o, final_g)


def kernel(x, norm_g, w_in, ret_gn_g, fox_b_f, pool_w, pool_scale,
           w_ret_branch, w_fox_branch, w_pool_branch, w_out, final_g):
    batch, seq, d = x.shape
    depth = w_in.shape[0]
    n_fox_cols = 8 * WIDTH
    tables = _retention_tables(seq)
    x2 = x.reshape(batch * seq, d)
    for layer in range(depth):
        w = w_in[layer]
        w_main = jnp.concatenate(
            [w[:, :n_fox_cols], w[:, n_fox_cols + N_HEADS:]], axis=1).astype(BF16)
        w_forget = jnp.pad(w[:, n_fox_cols:n_fox_cols + N_HEADS],
                           ((0, 0), (0, LANES - N_HEADS))).astype(BF16)
        bias = jnp.pad(fox_b_f[layer], (0, LANES - N_HEADS)).reshape(1, LANES)

        proj, f_logits = _inproj(x2, norm_g[layer].reshape(1, d), w_main, w_forget)
        y_ret = _retention(proj, tables, ret_gn_g[layer], batch, seq)
        c_col, c_row_t = _forget_cumsum(f_logits, bias, batch, seq)
        c_row = c_row_t[:, :N_HEADS, :].reshape(
            batch * N_HEADS, seq // FOX_BLOCK, 1, FOX_BLOCK)
        y_fox = _fox(proj, c_col, c_row, batch, seq, FOX_BLOCK)
        y_pool = _pool(proj, pool_w[layer], pool_scale[layer], batch, seq)
        x2 = _merge(x2, y_ret, y_fox, y_pool, proj,
                    w_ret_branch[layer].astype(BF16), w_fox_branch[layer].astype(BF16),
                    w_pool_branch[layer].astype(BF16), w_out[layer].astype(BF16),
                    final_g.reshape(1, d), final_norm=(layer == depth - 1))
    return x2.reshape(batch, seq, d)
```

```python
import functools

import jax
import jax.numpy as jnp
from jax import lax
from jax.experimental import pallas as pl
from jax.experimental.pallas import tpu as pltpu

D_MODEL = 2048
HEAD_DIM = 128
N_HEADS = 8
WIDTH = D_MODEL // 2
POOL_WINDOWS = (2, 4, 8, 16)
POOL_GROUP_DIM = WIDTH // len(POOL_WINDOWS)
CHUNK = 128
ROPE_THETA = 10000.0
EPS = 1e-6
NEG_INF = -1e30
LOG2E = 1.4426950408889634
FOX_BLOCK = 512

LANES = 128
N_MAIN = 10 * WIDTH + 3 * D_MODEL
COL_RET = 0
COL_FOX = 4 * N_HEADS
COL_POOL_U = 8 * N_HEADS
COL_POOL_Z = 9 * N_HEADS
COL_GATES = 10 * N_HEADS

BF16 = jnp.bfloat16
F32 = jnp.float32

VMEM_LIMIT = 56 * 1024 * 1024


def _params(sem, vmem=VMEM_LIMIT):
    return pltpu.CompilerParams(dimension_semantics=sem, vmem_limit_bytes=vmem)


def _inproj_kernel(x_ref, g_ref, w_ref, wf_ref, o_ref, f_ref, h_scr):
    @pl.when(pl.program_id(1) == 0)
    def _():
        x = x_ref[...]
        y = x * lax.rsqrt(jnp.mean(x * x, axis=-1, keepdims=True) + EPS)
        h = (y * g_ref[...]).astype(BF16)
        h_scr[...] = h
        f_ref[...] = jnp.dot(h, wf_ref[...], preferred_element_type=F32)

    o_ref[...] = jnp.dot(h_scr[...], w_ref[...], preferred_element_type=F32)


def _wprep_kernel(w_ref, main_ref, f_ref):
    n_a = 8 * WIDTH
    main_ref[:, :n_a] = w_ref[:, :n_a].astype(BF16)
    main_ref[:, n_a:] = w_ref[:, n_a + N_HEADS:].astype(BF16)
    lane = lax.broadcasted_iota(jnp.int32, f_ref.shape, 1)
    f_ref[...] = jnp.where(lane < N_HEADS, w_ref[:, n_a:n_a + LANES], 0.0).astype(BF16)


def _prep_w_in(w_in, tk=64):
    depth, d, c_in = w_in.shape
    return pl.pallas_call(
        _wprep_kernel,
        grid=(depth, d // tk),
        in_specs=[pl.BlockSpec((None, tk, c_in), lambda l, i: (l, i, 0))],
        out_specs=[
            pl.BlockSpec((None, tk, N_MAIN), lambda l, i: (l, i, 0)),
            pl.BlockSpec((None, tk, LANES), lambda l, i: (l, i, 0)),
        ],
        out_shape=[
            jax.ShapeDtypeStruct((depth, d, N_MAIN), BF16),
            jax.ShapeDtypeStruct((depth, d, LANES), BF16),
        ],
        compiler_params=_params(("parallel", "parallel")),
        name="w_in_prep",
    )(w_in)


def _inproj(x2, g, w_main, w_f, layer, tm=1024, tn=1024):
    m, d = x2.shape
    n = w_main.shape[2]
    return pl.pallas_call(
        _inproj_kernel,
        grid=(m // tm, n // tn),
        in_specs=[
            pl.BlockSpec((tm, d), lambda i, j: (i, 0)),
            pl.BlockSpec((1, d), lambda i, j: (0, 0)),
            pl.BlockSpec((None, d, tn), lambda i, j: (layer, 0, j)),
            pl.BlockSpec((None, d, LANES), lambda i, j: (layer, 0, 0)),
        ],
        out_specs=[
            pl.BlockSpec((tm, tn), lambda i, j: (i, j)),
            pl.BlockSpec((tm, LANES), lambda i, j: (i, 0)),
        ],
        out_shape=[
            jax.ShapeDtypeStruct((m, n), F32),
            jax.ShapeDtypeStruct((m, LANES), F32),
        ],
        scratch_shapes=[pltpu.VMEM((tm, d), BF16)],
        compiler_params=_params(("parallel", "arbitrary")),
        name="inproj",
    )(x2, g, w_main, w_f)


def _rotate(t, cos, sin_signed):
    return t * cos + pltpu.roll(t, HEAD_DIM // 2, 1) * sin_signed


def _retention_kernel(q_ref, k_ref, v_ref, z_ref, cos_ref, sin_ref, decay_ref,
                      xi_ref, zeta_ref, gch_ref, gn_ref, o_ref,
                      q_scr, k_scr, kz_scr, v_scr, inner_scr, state_scr, out_scr):
    decay = decay_ref[0]
    xi = xi_ref[0]
    zeta = zeta_ref[0]
    g_chunk = gch_ref[0, 0:1, :]
    n_chunks = q_ref.shape[0] // CHUNK
    chunk_rows = [pl.ds(c * CHUNK, CHUNK) for c in range(n_chunks)]
    nt = (((1,), (1,)), ((), ()))
    tn = (((0,), (0,)), ((), ()))

    cos = cos_ref[...]
    sin = sin_ref[...]
    q_scr[...] = _rotate(q_ref[...], cos, sin).astype(BF16)
    k = _rotate(k_ref[...], cos, sin) * (HEAD_DIM ** -0.5)
    k_scr[...] = k.astype(BF16)
    v_scr[...] = v_ref[...].astype(BF16)
    for c, rows in enumerate(chunk_rows):
        kz_scr[rows, :] = (k[c * CHUNK:(c + 1) * CHUNK] * zeta).astype(BF16)

    increments = [lax.dot_general(kz_scr[rows, :], v_scr[rows, :], tn, preferred_element_type=F32)
                  for rows in chunk_rows]
    for rows in chunk_rows:
        inner = lax.dot_general(q_scr[rows, :], k_scr[rows, :], nt,
                                preferred_element_type=F32) * decay
        inner_scr[rows, :] = inner.astype(BF16)

    state = jnp.zeros((HEAD_DIM, HEAD_DIM), F32)
    for rows, inc in zip(chunk_rows, increments):
        state_scr[rows, :] = state.astype(BF16)
        state = g_chunk * state + inc

    for rows in chunk_rows:
        out_scr[rows, :] = (
            jnp.dot(inner_scr[rows, :], v_scr[rows, :], preferred_element_type=F32)
            + jnp.dot(q_scr[rows, :], state_scr[rows, :], preferred_element_type=F32) * xi)

    out = out_scr[...]
    mu = jnp.mean(out, axis=-1, keepdims=True)
    dev = out - mu
    var = jnp.mean(dev * dev, axis=-1, keepdims=True)
    yn = dev * lax.rsqrt(var + EPS) * gn_ref[0]
    z = z_ref[...]
    o_ref[...] = (yn * (z * jax.nn.sigmoid(z))).astype(o_ref.dtype)


def _retention(proj, tables, gn_g, batch, seq):
    cos, sin, decay, xi, zeta, gch = tables
    m = proj.shape[0]
    head_block = lambda off: pl.BlockSpec((seq, HEAD_DIM), lambda b, h: (b, off + h))
    per_head = lambda shape: pl.BlockSpec((1,) + shape, lambda b, h: (h, 0, 0))
    return pl.pallas_call(
        _retention_kernel,
        grid=(batch, N_HEADS),
        in_specs=[
            head_block(COL_RET), head_block(COL_RET + N_HEADS),
            head_block(COL_RET + 2 * N_HEADS), head_block(COL_RET + 3 * N_HEADS),
            pl.BlockSpec((seq, HEAD_DIM), lambda b, h: (0, 0)),
            pl.BlockSpec((seq, HEAD_DIM), lambda b, h: (0, 0)),
            per_head((CHUNK, CHUNK)), per_head((CHUNK, HEAD_DIM)),
            per_head((CHUNK, HEAD_DIM)), per_head((8, HEAD_DIM)),
            per_head((1, HEAD_DIM)),
        ],
        out_specs=pl.BlockSpec((seq, HEAD_DIM), lambda b, h: (b, h)),
        out_shape=jax.ShapeDtypeStruct((m, WIDTH), BF16),
        scratch_shapes=[pltpu.VMEM((seq, HEAD_DIM), BF16)] * 6 + [pltpu.VMEM((seq, HEAD_DIM), F32)],
        compiler_params=_params(("parallel", "arbitrary")),
        name="retention",
    )(proj, proj, proj, proj, cos, sin, decay, xi, zeta, gch,
      gn_g.reshape(N_HEADS, 1, HEAD_DIM))


def _retention_tables(seq):
    half = HEAD_DIM // 2
    inv = ROPE_THETA ** (-jnp.arange(half, dtype=F32) / half)
    ang = jnp.arange(seq, dtype=F32)[:, None] * inv[None, :]
    cos = jnp.cos(ang)
    sin = jnp.sin(ang)
    cos2 = jnp.concatenate([cos, cos], axis=-1)
    sin2 = jnp.concatenate([-sin, sin], axis=-1)
    log_g = jnp.log1p(-jnp.exp2(-5.0 - jnp.arange(N_HEADS, dtype=F32)))
    idx = jnp.arange(CHUNK, dtype=F32)
    diff = idx[:, None] - idx[None, :]
    decay = jnp.where(diff >= 0, jnp.exp(log_g[:, None, None] * jnp.maximum(diff, 0.0)), 0.0)
    xi = jnp.exp(log_g[:, None] * (idx + 1.0))
    zeta = jnp.exp(log_g[:, None] * (CHUNK - 1.0 - idx))
    g_chunk = jnp.exp(log_g * CHUNK)
    bcast = lambda t: jnp.broadcast_to(t[:, :, None], (N_HEADS, CHUNK, HEAD_DIM))
    gch = jnp.broadcast_to(g_chunk[:, None, None], (N_HEADS, 8, HEAD_DIM))
    return cos2, sin2, decay, bcast(xi), bcast(zeta), gch


def _split3(x):
    hi = x.astype(BF16)
    r1 = x - hi.astype(F32)
    mid = r1.astype(BF16)
    lo = (r1 - mid.astype(F32)).astype(BF16)
    return hi, mid, lo


def _forget_cumsum_kernel(f_ref, b_ref, col_ref, row_ref):
    n_blocks = f_ref.shape[0] // CHUNK
    r = lax.broadcasted_iota(jnp.int32, (CHUNK, CHUNK), 0)
    c = lax.broadcasted_iota(jnp.int32, (CHUNK, CHUNK), 1)
    tri = jnp.where(r >= c, 1.0, 0.0).astype(BF16)
    carry = jnp.zeros((1, LANES), F32)
    for blk in range(n_blocks):
        rows = pl.ds(blk * CHUNK, CHUNK)
        x = f_ref[rows, :] + b_ref[...]
        ls = jnp.minimum(x, 0.0) - jnp.log1p(jnp.exp(-jnp.abs(x)))
        hi, mid, lo = _split3(ls)
        cs = (jnp.dot(tri, hi, preferred_element_type=F32)
              + jnp.dot(tri, mid, preferred_element_type=F32)
              + jnp.dot(tri, lo, preferred_element_type=F32)) + carry
        col_ref[rows, :] = cs
        row_ref[0, :, rows] = cs.T
        carry = cs[CHUNK - 1:CHUNK, :]


def _forget_cumsum(f_logits, bias, batch, seq):
    m = f_logits.shape[0]
    return pl.pallas_call(
        _forget_cumsum_kernel,
        grid=(batch,),
        in_specs=[
            pl.BlockSpec((seq, LANES), lambda b: (b, 0)),
            pl.BlockSpec((1, LANES), lambda b: (0, 0)),
        ],
        out_specs=[
            pl.BlockSpec((seq, LANES), lambda b: (b, 0)),
            pl.BlockSpec((1, LANES, seq), lambda b: (b, 0, 0)),
        ],
        out_shape=[
            jax.ShapeDtypeStruct((m, LANES), F32),
            jax.ShapeDtypeStruct((batch, LANES, seq), F32),
        ],
        compiler_params=_params(("parallel",)),
        name="forget_cumsum",
    )(f_logits, bias)


def _lane_groups(x):
    return [x[:, g * LANES:(g + 1) * LANES] for g in range(x.shape[1] // LANES)]


def _fox_kernel(q_ref, k_ref, v_ref, z_ref, ccol_ref, crow_ref, o_ref,
                m_scr, acc_scr, *, tq):
    h = pl.program_id(1)
    seq = q_ref.shape[0]
    c1 = (HEAD_DIM ** -0.5) * LOG2E
    ones = jnp.ones((tq, HEAD_DIM), BF16)

    qb = q_ref[...].astype(BF16)
    lane = lax.broadcasted_iota(jnp.int32, (seq, LANES), 1)
    cq2 = jnp.sum(jnp.where(lane == h, ccol_ref[...], 0.0), axis=-1, keepdims=True) * LOG2E
    causal = (lax.broadcasted_iota(jnp.int32, (tq, tq), 0)
              >= lax.broadcasted_iota(jnp.int32, (tq, tq), 1))

    for j in range(seq // tq):
        lo = j * tq
        live = slice(lo, seq)
        kj = k_ref[lo:lo + tq, :].astype(BF16)
        vj = jnp.concatenate([v_ref[lo:lo + tq, :].astype(BF16), ones], axis=-1)
        s = lax.dot_general(qb[live], kj, (((1,), (1,)), ((), ())), preferred_element_type=F32)
        a = s * c1 - crow_ref[0, j] * LOG2E
        diag = jnp.where(causal, a[:tq], NEG_INF)
        a = diag if lo + tq == seq else jnp.concatenate([diag, a[tq:]], axis=0)
        groups = _lane_groups(a)
        m_blk = jnp.max(functools.reduce(jnp.maximum, groups), axis=-1, keepdims=True) + cq2[live]
        if j == 0:
            m_new = jnp.broadcast_to(m_blk, (seq, LANES))
        else:
            m_prev = m_scr[live, :]
            m_new = jnp.maximum(m_prev, m_blk)
            alpha = jnp.exp2(m_prev - m_new)
        shift = m_new - cq2[live]
        p = jnp.concatenate([jnp.exp2(g - shift).astype(BF16) for g in groups], axis=-1)
        pv = jnp.dot(p, vj, preferred_element_type=F32)
        m_scr[live, :] = m_new
        if j == 0:
            acc_scr[...] = pv
        else:
            acc_scr[live, :] = jnp.concatenate([alpha, alpha], axis=-1) * acc_scr[live, :] + pv

    z = z_ref[...]
    o_ref[...] = ((acc_scr[:, :HEAD_DIM] / acc_scr[:, HEAD_DIM:])
                  * (z * jax.nn.sigmoid(z))).astype(o_ref.dtype)


def _fox(proj, c_col, c_row, batch, seq, tq):
    m = proj.shape[0]
    head_block = lambda off: pl.BlockSpec((seq, HEAD_DIM), lambda b, h: (b, off + h))
    return pl.pallas_call(
        functools.partial(_fox_kernel, tq=tq),
        grid=(batch, N_HEADS),
        in_specs=[
            head_block(COL_FOX), head_block(COL_FOX + N_HEADS),
            head_block(COL_FOX + 2 * N_HEADS), head_block(COL_FOX + 3 * N_HEADS),
            pl.BlockSpec((seq, LANES), lambda b, h: (b, 0)),
            pl.BlockSpec((1, seq // tq, 1, tq), lambda b, h: (b * N_HEADS + h, 0, 0, 0)),
        ],
        out_specs=pl.BlockSpec((seq, HEAD_DIM), lambda b, h: (b, h)),
        out_shape=jax.ShapeDtypeStruct((m, WIDTH), BF16),
        scratch_shapes=[pltpu.VMEM((seq, LANES), F32), pltpu.VMEM((seq, 2 * HEAD_DIM), F32)],
        compiler_params=_params(("parallel", "arbitrary")),
        name="fox_attention",
    )(proj, proj, proj, proj, c_col, c_row)


def _pool_kernel(u_ref, z_ref, w_ref, s_ref, o_ref, pooled_scr):
    seq = u_ref.shape[0]
    g = pl.program_id(1)
    sublanes = 8
    head_row = lax.broadcasted_iota(jnp.int32, (sublanes, POOL_GROUP_DIM), 0)
    t1 = (lax.broadcasted_iota(jnp.int32, (seq, LANES), 0) + 1).astype(F32)

    def shifted(x, k):
        r = pltpu.roll(x, k, 0)
        head = jnp.where(head_row >= k, r[:sublanes], 0.0)
        return jnp.concatenate([head, r[sublanes:]], axis=0)

    for gi, w in enumerate(POOL_WINDOWS):
        @pl.when(g == gi)
        def _():
            u = u_ref[...]
            total, k = u, 1
            while k < w:
                total = total + shifted(total, k)
                k *= 2
            inv = 1.0 / jnp.minimum(t1, float(w))
            mean = jnp.concatenate([x * inv for x in _lane_groups(total)], axis=-1)
            pooled_scr[...] = (mean - u).astype(BF16)

    mixed = jnp.dot(pooled_scr[...], w_ref[0], preferred_element_type=F32) * s_ref[0]
    z = z_ref[...]
    o_ref[...] = (mixed * (z * jax.nn.sigmoid(z))).astype(o_ref.dtype)


def _pool(proj, pool_w, pool_scale, batch, seq):
    m = proj.shape[0]
    n_groups = len(POOL_WINDOWS)
    blocks_per_group = POOL_GROUP_DIM // LANES
    grp = lambda off: pl.BlockSpec(
        (seq, POOL_GROUP_DIM), lambda b, g: (b, off // blocks_per_group + g))
    return pl.pallas_call(
        _pool_kernel,
        grid=(batch, n_groups),
        in_specs=[
            grp(COL_POOL_U), grp(COL_POOL_Z),
            pl.BlockSpec((1, POOL_GROUP_DIM, POOL_GROUP_DIM), lambda b, g: (g, 0, 0)),
            pl.BlockSpec((1, 1, POOL_GROUP_DIM), lambda b, g: (g, 0, 0)),
        ],
        out_specs=pl.BlockSpec((seq, POOL_GROUP_DIM), lambda b, g: (b, g)),
        out_shape=jax.ShapeDtypeStruct((m, WIDTH), BF16),
        scratch_shapes=[pltpu.VMEM((seq, POOL_GROUP_DIM), BF16)],
        compiler_params=_params(("parallel", "arbitrary")),
        name="pool_mixer",
    )(proj, proj, pool_w.astype(BF16), pool_scale.reshape(n_groups, 1, POOL_GROUP_DIM))


def _merge_kernel(x_ref, yr_ref, yf_ref, yp_ref, ga_ref, gb_ref, gc_ref,
                  wr_ref, wf_ref, wp_ref, wo_ref, fg_ref, o_ref, *, final_norm):
    merged = (jax.nn.sigmoid(ga_ref[...]) * jnp.dot(yr_ref[...], wr_ref[...], preferred_element_type=F32)
              + jax.nn.sigmoid(gb_ref[...]) * jnp.dot(yf_ref[...], wf_ref[...], preferred_element_type=F32)
              + jax.nn.sigmoid(gc_ref[...]) * jnp.dot(yp_ref[...], wp_ref[...], preferred_element_type=F32))
    x = x_ref[...] + jnp.dot(merged.astype(BF16), wo_ref[...], preferred_element_type=F32)
    if final_norm:
        x = x * lax.rsqrt(jnp.mean(x * x, axis=-1, keepdims=True) + EPS) * fg_ref[...]
    o_ref[...] = x


def _merge(x2, y_ret, y_fox, y_pool, proj, w_r, w_f, w_p, w_o, final_g, layer, final_norm, tm=256):
    m, d = x2.shape
    gate_blocks = d // LANES
    row = lambda width: pl.BlockSpec((tm, width), lambda i: (i, 0))
    gate = lambda k: pl.BlockSpec((tm, d), lambda i: (i, COL_GATES // gate_blocks + k))
    weight = lambda rows: pl.BlockSpec((None, rows, d), lambda i: (layer, 0, 0),
                                       pipeline_mode=pl.Buffered(1))
    return pl.pallas_call(
        functools.partial(_merge_kernel, final_norm=final_norm),
        grid=(m // tm,),
        in_specs=[
            row(d), row(WIDTH), row(WIDTH), row(WIDTH), gate(0), gate(1), gate(2),
            weight(WIDTH), weight(WIDTH), weight(WIDTH), weight(d),
            pl.BlockSpec((1, d), lambda i: (0, 0), pipeline_mode=pl.Buffered(1)),
        ],
        out_specs=row(d),
        out_shape=jax.ShapeDtypeStruct((m, d), F32),
        compiler_params=_params(("parallel",)),
        name="merge_out",
    )(x2, y_ret, y_fox, y_pool, proj, proj, proj, w_r, w_f, w_p, w_o, final_g)


def kernel(x, norm_g, w_in, ret_gn_g, fox_b_f, pool_w, pool_scale,
           w_ret_branch, w_fox_branch, w_pool_branch, w_out, final_g):
    batch, seq, d = x.shape
    depth = w_in.shape[0]
    tables = _retention_tables(seq)
    w_main, w_forget = _prep_w_in(w_in)
    w_r, w_f, w_p, w_o = (t.astype(BF16) for t in (w_ret_branch, w_fox_branch, w_pool_branch, w_out))
    x2 = x.reshape(batch * seq, d)
    for layer in range(depth):
        bias = jnp.pad(fox_b_f[layer], (0, LANES - N_HEADS)).reshape(1, LANES)

        proj, f_logits = _inproj(x2, norm_g[layer].reshape(1, d), w_main, w_forget, layer)
        y_ret = _retention(proj, tables, ret_gn_g[layer], batch, seq)
        c_col, c_row_t = _forget_cumsum(f_logits, bias, batch, seq)
        c_row = c_row_t[:, :N_HEADS, :].reshape(
            batch * N_HEADS, seq // FOX_BLOCK, 1, FOX_BLOCK)
        y_fox = _fox(proj, c_col, c_row, batch, seq, FOX_BLOCK)
        y_pool = _pool(proj, pool_w[layer], pool_scale[layer], batch, seq)
        x2 = _merge(x2, y_ret, y_fox, y_pool, proj, w_r, w_f, w_p, w_o,
                    final_g.reshape(1, d), layer, final_norm=(layer == depth - 1))
    return x2.reshape(batch, seq, d)
```

```python
import functools

import jax
import jax.numpy as jnp
from jax import lax
from jax.experimental import pallas as pl
from jax.experimental.pallas import tpu as pltpu

D_MODEL = 2048
HEAD_DIM = 128
N_HEADS = 8
WIDTH = D_MODEL // 2
POOL_WINDOWS = (2, 4, 8, 16)
POOL_GROUP_DIM = WIDTH // len(POOL_WINDOWS)
CHUNK = 128
ROPE_THETA = 10000.0
EPS = 1e-6
NEG_INF = -1e30
LOG2E = 1.4426950408889634
FOX_BLOCK = 512

LANES = 128
N_MAIN = 10 * WIDTH + 3 * D_MODEL
COL_RET = 0
COL_FOX = 4 * N_HEADS
COL_POOL_U = 8 * N_HEADS
COL_POOL_Z = 9 * N_HEADS
COL_GATES = 10 * N_HEADS

BF16 = jnp.bfloat16
F32 = jnp.float32
NT_DIMS = (((1,), (1,)), ((), ()))

VMEM_LIMIT = 56 * 1024 * 1024


def _params(sem, vmem=VMEM_LIMIT):
    return pltpu.CompilerParams(dimension_semantics=sem, vmem_limit_bytes=vmem)


def _inproj_kernel(x_ref, g_ref, w_ref, wf_ref, o_ref, f_ref, h_scr):
    @pl.when(pl.program_id(1) == 0)
    def _():
        x = x_ref[...]
        y = x * lax.rsqrt(jnp.mean(x * x, axis=-1, keepdims=True) + EPS)
        h = (y * g_ref[...]).astype(BF16)
        h_scr[...] = h
        f_ref[...] = lax.dot_general(h, wf_ref[...], NT_DIMS, preferred_element_type=F32)

    o_ref[...] = lax.dot_general(h_scr[...], w_ref[...], NT_DIMS, preferred_element_type=F32)


def _wprep_kernel(main_ref, tail_ref, out_ref, *, rows_before_gap):
    tr = out_ref.shape[0]
    gap = tail_ref.shape[0]
    i = pl.program_id(1)

    @pl.when(i * tr < rows_before_gap)
    def _():
        out_ref[...] = main_ref[...].astype(BF16)

    @pl.when(i * tr >= rows_before_gap)
    def _():
        out_ref[:tr - gap, :] = main_ref[gap:, :].astype(BF16)
        out_ref[tr - gap:, :] = tail_ref[...].astype(BF16)


def _prep_w_in(w_in_t, tr=512):
    depth, _, d = w_in_t.shape
    return pl.pallas_call(
        functools.partial(_wprep_kernel, rows_before_gap=8 * WIDTH),
        grid=(depth, N_MAIN // tr),
        in_specs=[
            pl.BlockSpec((None, tr, d), lambda l, i: (l, i, 0)),
            pl.BlockSpec((None, N_HEADS, d), lambda l, i: (l, (i + 1) * (tr // N_HEADS), 0)),
        ],
        out_specs=pl.BlockSpec((None, tr, d), lambda l, i: (l, i, 0)),
        out_shape=jax.ShapeDtypeStruct((depth, N_MAIN, d), BF16),
        compiler_params=_params(("parallel", "parallel")),
        name="w_in_prep",
    )(w_in_t, w_in_t)


def _inproj(x2, g, w_main_t, w_f_t, layer, tm=1024, tn=1024):
    m, d = x2.shape
    n = w_main_t.shape[1]
    return pl.pallas_call(
        _inproj_kernel,
        grid=(m // tm, n // tn),
        in_specs=[
            pl.BlockSpec((tm, d), lambda i, j: (i, 0)),
            pl.BlockSpec((1, d), lambda i, j: (0, 0)),
            pl.BlockSpec((None, tn, d), lambda i, j: (layer, j, 0)),
            pl.BlockSpec((None, LANES, d), lambda i, j: (layer, 0, 0)),
        ],
        out_specs=[
            pl.BlockSpec((tm, tn), lambda i, j: (i, j)),
            pl.BlockSpec((tm, LANES), lambda i, j: (i, 0)),
        ],
        out_shape=[
            jax.ShapeDtypeStruct((m, n), F32),
            jax.ShapeDtypeStruct((m, LANES), F32),
        ],
        scratch_shapes=[pltpu.VMEM((tm, d), BF16)],
        compiler_params=_params(("parallel", "arbitrary")),
        name="inproj",
    )(x2, g, w_main_t, w_f_t)


def _rotate(t, cos, sin_signed):
    return t * cos + pltpu.roll(t, HEAD_DIM // 2, 1) * sin_signed


def _retention_kernel(q_ref, k_ref, v_ref, z_ref, cos_ref, sin_ref, decay_ref,
                      xi_ref, zeta_ref, gch_ref, gn_ref, o_ref,
                      q_scr, k_scr, kz_scr, v_scr, inner_scr, state_scr, out_scr):
    decay = decay_ref[0]
    xi = xi_ref[0]
    zeta = zeta_ref[0]
    g_chunk = gch_ref[0, 0:1, :]
    n_chunks = q_ref.shape[0] // CHUNK
    chunk_rows = [pl.ds(c * CHUNK, CHUNK) for c in range(n_chunks)]
    nt = (((1,), (1,)), ((), ()))
    tn = (((0,), (0,)), ((), ()))

    cos = cos_ref[...]
    sin = sin_ref[...]
    q_scr[...] = _rotate(q_ref[...], cos, sin).astype(BF16)
    k = _rotate(k_ref[...], cos, sin) * (HEAD_DIM ** -0.5)
    k_scr[...] = k.astype(BF16)
    v_scr[...] = v_ref[...].astype(BF16)
    for c, rows in enumerate(chunk_rows):
        kz_scr[rows, :] = (k[c * CHUNK:(c + 1) * CHUNK] * zeta).astype(BF16)

    increments = [lax.dot_general(kz_scr[rows, :], v_scr[rows, :], tn, preferred_element_type=F32)
                  for rows in chunk_rows]
    for rows in chunk_rows:
        inner = lax.dot_general(q_scr[rows, :], k_scr[rows, :], nt,
                                preferred_element_type=F32) * decay
        inner_scr[rows, :] = inner.astype(BF16)

    state = jnp.zeros((HEAD_DIM, HEAD_DIM), F32)
    for rows, inc in zip(chunk_rows, increments):
        state_scr[rows, :] = state.astype(BF16)
        state = g_chunk * state + inc

    for rows in chunk_rows:
        out_scr[rows, :] = (
            jnp.dot(inner_scr[rows, :], v_scr[rows, :], preferred_element_type=F32)
            + jnp.dot(q_scr[rows, :], state_scr[rows, :], preferred_element_type=F32) * xi)

    out = out_scr[...]
    mu = jnp.mean(out, axis=-1, keepdims=True)
    dev = out - mu
    var = jnp.mean(dev * dev, axis=-1, keepdims=True)
    yn = dev * lax.rsqrt(var + EPS) * gn_ref[0]
    z = z_ref[...]
    o_ref[...] = (yn * (z * jax.nn.sigmoid(z))).astype(o_ref.dtype)


def _retention(proj, tables, gn_g, batch, seq):
    cos, sin, decay, xi, zeta, gch = tables
    m = proj.shape[0]
    head_block = lambda off: pl.BlockSpec((seq, HEAD_DIM), lambda b, h: (b, off + h))
    per_head = lambda shape: pl.BlockSpec((1,) + shape, lambda b, h: (h, 0, 0))
    return pl.pallas_call(
        _retention_kernel,
        grid=(batch, N_HEADS),
        in_specs=[
            head_block(COL_RET), head_block(COL_RET + N_HEADS),
            head_block(COL_RET + 2 * N_HEADS), head_block(COL_RET + 3 * N_HEADS),
            pl.BlockSpec((seq, HEAD_DIM), lambda b, h: (0, 0)),
            pl.BlockSpec((seq, HEAD_DIM), lambda b, h: (0, 0)),
            per_head((CHUNK, CHUNK)), per_head((CHUNK, HEAD_DIM)),
            per_head((CHUNK, HEAD_DIM)), per_head((8, HEAD_DIM)),
            per_head((1, HEAD_DIM)),
        ],
        out_specs=pl.BlockSpec((seq, HEAD_DIM), lambda b, h: (b, h)),
        out_shape=jax.ShapeDtypeStruct((m, WIDTH), BF16),
        scratch_shapes=[pltpu.VMEM((seq, HEAD_DIM), BF16)] * 6 + [pltpu.VMEM((seq, HEAD_DIM), F32)],
        compiler_params=_params(("parallel", "arbitrary")),
        name="retention",
    )(proj, proj, proj, proj, cos, sin, decay, xi, zeta, gch,
      gn_g.reshape(N_HEADS, 1, HEAD_DIM))


def _retention_tables(seq):
    half = HEAD_DIM // 2
    inv = ROPE_THETA ** (-jnp.arange(half, dtype=F32) / half)
    ang = jnp.arange(seq, dtype=F32)[:, None] * inv[None, :]
    cos = jnp.cos(ang)
    sin = jnp.sin(ang)
    cos2 = jnp.concatenate([cos, cos], axis=-1)
    sin2 = jnp.concatenate([-sin, sin], axis=-1)
    log_g = jnp.log1p(-jnp.exp2(-5.0 - jnp.arange(N_HEADS, dtype=F32)))
    idx = jnp.arange(CHUNK, dtype=F32)
    diff = idx[:, None] - idx[None, :]
    decay = jnp.where(diff >= 0, jnp.exp(log_g[:, None, None] * jnp.maximum(diff, 0.0)), 0.0)
    xi = jnp.exp(log_g[:, None] * (idx + 1.0))
    zeta = jnp.exp(log_g[:, None] * (CHUNK - 1.0 - idx))
    g_chunk = jnp.exp(log_g * CHUNK)
    bcast = lambda t: jnp.broadcast_to(t[:, :, None], (N_HEADS, CHUNK, HEAD_DIM))
    gch = jnp.broadcast_to(g_chunk[:, None, None], (N_HEADS, 8, HEAD_DIM))
    return cos2, sin2, decay, bcast(xi), bcast(zeta), gch


def _split3(x):
    hi = x.astype(BF16)
    r1 = x - hi.astype(F32)
    mid = r1.astype(BF16)
    lo = (r1 - mid.astype(F32)).astype(BF16)
    return hi, mid, lo


def _forget_cumsum_kernel(f_ref, b_ref, col_ref, row_ref):
    n_blocks = f_ref.shape[0] // CHUNK
    r = lax.broadcasted_iota(jnp.int32, (CHUNK, CHUNK), 0)
    c = lax.broadcasted_iota(jnp.int32, (CHUNK, CHUNK), 1)
    tri = jnp.where(r >= c, 1.0, 0.0).astype(BF16)
    carry = jnp.zeros((1, LANES), F32)
    for blk in range(n_blocks):
        rows = pl.ds(blk * CHUNK, CHUNK)
        x = f_ref[rows, :] + b_ref[...]
        ls = jnp.minimum(x, 0.0) - jnp.log1p(jnp.exp(-jnp.abs(x)))
        hi, mid, lo = _split3(ls)
        cs = (jnp.dot(tri, hi, preferred_element_type=F32)
              + jnp.dot(tri, mid, preferred_element_type=F32)
              + jnp.dot(tri, lo, preferred_element_type=F32)) + carry
        col_ref[rows, :] = cs
        row_ref[0, :, rows] = cs.T
        carry = cs[CHUNK - 1:CHUNK, :]


def _forget_cumsum(f_logits, bias, batch, seq):
    m = f_logits.shape[0]
    return pl.pallas_call(
        _forget_cumsum_kernel,
        grid=(batch,),
        in_specs=[
            pl.BlockSpec((seq, LANES), lambda b: (b, 0)),
            pl.BlockSpec((1, LANES), lambda b: (0, 0)),
        ],
        out_specs=[
            pl.BlockSpec((seq, LANES), lambda b: (b, 0)),
            pl.BlockSpec((1, LANES, seq), lambda b: (b, 0, 0)),
        ],
        out_shape=[
            jax.ShapeDtypeStruct((m, LANES), F32),
            jax.ShapeDtypeStruct((batch, LANES, seq), F32),
        ],
        compiler_params=_params(("parallel",)),
        name="forget_cumsum",
    )(f_logits, bias)


def _lane_groups(x):
    return [x[:, g * LANES:(g + 1) * LANES] for g in range(x.shape[1] // LANES)]


def _fox_kernel(q_ref, k_ref, v_ref, z_ref, ccol_ref, crow_ref, o_ref,
                m_scr, acc_scr, *, tq):
    h = pl.program_id(1)
    seq = q_ref.shape[0]
    c1 = (HEAD_DIM ** -0.5) * LOG2E
    ones = jnp.ones((tq, HEAD_DIM), BF16)

    qb = q_ref[...].astype(BF16)
    lane = lax.broadcasted_iota(jnp.int32, (seq, LANES), 1)
    cq2 = jnp.sum(jnp.where(lane == h, ccol_ref[...], 0.0), axis=-1, keepdims=True) * LOG2E
    causal = (lax.broadcasted_iota(jnp.int32, (tq, tq), 0)
              >= lax.broadcasted_iota(jnp.int32, (tq, tq), 1))

    for j in range(seq // tq):
        lo = j * tq
        live = slice(lo, seq)
        kj = k_ref[lo:lo + tq, :].astype(BF16)
        vj = jnp.concatenate([v_ref[lo:lo + tq, :].astype(BF16), ones], axis=-1)
        s = lax.dot_general(qb[live], kj, (((1,), (1,)), ((), ())), preferred_element_type=F32)
        a = s * c1 - crow_ref[0, j] * LOG2E
        diag = jnp.where(causal, a[:tq], NEG_INF)
        a = diag if lo + tq == seq else jnp.concatenate([diag, a[tq:]], axis=0)
        groups = _lane_groups(a)
        m_blk = jnp.max(functools.reduce(jnp.maximum, groups), axis=-1, keepdims=True) + cq2[live]
        if j == 0:
            m_new = jnp.broadcast_to(m_blk, (seq, LANES))
        else:
            m_prev = m_scr[live, :]
            m_new = jnp.maximum(m_prev, m_blk)
            alpha = jnp.exp2(m_prev - m_new)
        shift = m_new - cq2[live]
        p = jnp.concatenate([jnp.exp2(g - shift).astype(BF16) for g in groups], axis=-1)
        pv = jnp.dot(p, vj, preferred_element_type=F32)
        m_scr[live, :] = m_new
        if j == 0:
            acc_scr[...] = pv
        else:
            acc_scr[live, :] = jnp.concatenate([alpha, alpha], axis=-1) * acc_scr[live, :] + pv

    z = z_ref[...]
    o_ref[...] = ((acc_scr[:, :HEAD_DIM] / acc_scr[:, HEAD_DIM:])
                  * (z * jax.nn.sigmoid(z))).astype(o_ref.dtype)


def _fox(proj, c_col, c_row, batch, seq, tq):
    m = proj.shape[0]
    head_block = lambda off: pl.BlockSpec((seq, HEAD_DIM), lambda b, h: (b, off + h))
    return pl.pallas_call(
        functools.partial(_fox_kernel, tq=tq),
        grid=(batch, N_HEADS),
        in_specs=[
            head_block(COL_FOX), head_block(COL_FOX + N_HEADS),
            head_block(COL_FOX + 2 * N_HEADS), head_block(COL_FOX + 3 * N_HEADS),
            pl.BlockSpec((seq, LANES), lambda b, h: (b, 0)),
            pl.BlockSpec((1, seq // tq, 1, tq), lambda b, h: (b * N_HEADS + h, 0, 0, 0)),
        ],
        out_specs=pl.BlockSpec((seq, HEAD_DIM), lambda b, h: (b, h)),
        out_shape=jax.ShapeDtypeStruct((m, WIDTH), BF16),
        scratch_shapes=[pltpu.VMEM((seq, LANES), F32), pltpu.VMEM((seq, 2 * HEAD_DIM), F32)],
        compiler_params=_params(("parallel", "arbitrary")),
        name="fox_attention",
    )(proj, proj, proj, proj, c_col, c_row)


def _pool_kernel(u_ref, z_ref, w_ref, s_ref, o_ref, pooled_scr):
    seq = u_ref.shape[0]
    g = pl.program_id(1)
    sublanes = 8
    head_row = lax.broadcasted_iota(jnp.int32, (sublanes, POOL_GROUP_DIM), 0)
    t1 = (lax.broadcasted_iota(jnp.int32, (seq, LANES), 0) + 1).astype(F32)

    def shifted(x, k):
        r = pltpu.roll(x, k, 0)
        head = jnp.where(head_row >= k, r[:sublanes], 0.0)
        return jnp.concatenate([head, r[sublanes:]], axis=0)

    for gi, w in enumerate(POOL_WINDOWS):
        @pl.when(g == gi)
        def _():
            u = u_ref[...]
            total, k = u, 1
            while k < w:
                total = total + shifted(total, k)
                k *= 2
            inv = 1.0 / jnp.minimum(t1, float(w))
            mean = jnp.concatenate([x * inv for x in _lane_groups(total)], axis=-1)
            pooled_scr[...] = (mean - u).astype(BF16)

    mixed = jnp.dot(pooled_scr[...], w_ref[0], preferred_element_type=F32) * s_ref[0]
    z = z_ref[...]
    o_ref[...] = (mixed * (z * jax.nn.sigmoid(z))).astype(o_ref.dtype)


def _pool(proj, pool_w, pool_scale, batch, seq):
    m = proj.shape[0]
    n_groups = len(POOL_WINDOWS)
    blocks_per_group = POOL_GROUP_DIM // LANES
    grp = lambda off: pl.BlockSpec(
        (seq, POOL_GROUP_DIM), lambda b, g: (b, off // blocks_per_group + g))
    return pl.pallas_call(
        _pool_kernel,
        grid=(batch, n_groups),
        in_specs=[
            grp(COL_POOL_U), grp(COL_POOL_Z),
            pl.BlockSpec((1, POOL_GROUP_DIM, POOL_GROUP_DIM), lambda b, g: (g, 0, 0)),
            pl.BlockSpec((1, 1, POOL_GROUP_DIM), lambda b, g: (g, 0, 0)),
        ],
        out_specs=pl.BlockSpec((seq, POOL_GROUP_DIM), lambda b, g: (b, g)),
        out_shape=jax.ShapeDtypeStruct((m, WIDTH), BF16),
        scratch_shapes=[pltpu.VMEM((seq, POOL_GROUP_DIM), BF16)],
        compiler_params=_params(("parallel", "arbitrary")),
        name="pool_mixer",
    )(proj, proj, pool_w.astype(BF16), pool_scale.reshape(n_groups, 1, POOL_GROUP_DIM))


def _merge_kernel(x_ref, yr_ref, yf_ref, yp_ref, ga_ref, gb_ref, gc_ref,
                  wr_ref, wf_ref, wp_ref, wo_ref, fg_ref, o_ref, *, final_norm):
    merged = (jax.nn.sigmoid(ga_ref[...]) * jnp.dot(yr_ref[...], wr_ref[...], preferred_element_type=F32)
              + jax.nn.sigmoid(gb_ref[...]) * jnp.dot(yf_ref[...], wf_ref[...], preferred_element_type=F32)
              + jax.nn.sigmoid(gc_ref[...]) * jnp.dot(yp_ref[...], wp_ref[...], preferred_element_type=F32))
    x = x_ref[...] + jnp.dot(merged.astype(BF16), wo_ref[...], preferred_element_type=F32)
    if final_norm:
        x = x * lax.rsqrt(jnp.mean(x * x, axis=-1, keepdims=True) + EPS) * fg_ref[...]
    o_ref[...] = x


def _merge(x2, y_ret, y_fox, y_pool, proj, w_r, w_f, w_p, w_o, final_g, layer, final_norm, tm=256):
    m, d = x2.shape
    gate_blocks = d // LANES
    row = lambda width: pl.BlockSpec((tm, width), lambda i: (i, 0))
    gate = lambda k: pl.BlockSpec((tm, d), lambda i: (i, COL_GATES // gate_blocks + k))
    weight = lambda rows: pl.BlockSpec((None, rows, d), lambda i: (layer, 0, 0),
                                       pipeline_mode=pl.Buffered(1))
    return pl.pallas_call(
        functools.partial(_merge_kernel, final_norm=final_norm),
        grid=(m // tm,),
        in_specs=[
            row(d), row(WIDTH), row(WIDTH), row(WIDTH), gate(0), gate(1), gate(2),
            weight(WIDTH), weight(WIDTH), weight(WIDTH), weight(d),
            pl.BlockSpec((1, d), lambda i: (0, 0), pipeline_mode=pl.Buffered(1)),
        ],
        out_specs=row(d),
        out_shape=jax.ShapeDtypeStruct((m, d), F32),
        compiler_params=_params(("parallel",)),
        name="merge_out",
    )(x2, y_ret, y_fox, y_pool, proj, proj, proj, w_r, w_f, w_p, w_o, final_g)


def kernel(x, norm_g, w_in, ret_gn_g, fox_b_f, pool_w, pool_scale,
           w_ret_branch, w_fox_branch, w_pool_branch, w_out, final_g):
    batch, seq, d = x.shape
    depth = w_in.shape[0]
    tables = _retention_tables(seq)
    n_before = 8 * WIDTH
    w_in_t = jnp.swapaxes(w_in, 1, 2)
    w_main = _prep_w_in(w_in_t)
    w_forget = jnp.pad(w_in_t[:, n_before:n_before + N_HEADS, :],
                       ((0, 0), (0, LANES - N_HEADS), (0, 0))).astype(BF16)
    w_r, w_f, w_p, w_o = (t.astype(BF16) for t in (w_ret_branch, w_fox_branch, w_pool_branch, w_out))
    x2 = x.reshape(batch * seq, d)
    for layer in range(depth):
        bias = jnp.pad(fox_b_f[layer], (0, LANES - N_HEADS)).reshape(1, LANES)

        proj, f_logits = _inproj(x2, norm_g[layer].reshape(1, d), w_main, w_forget, layer)
        y_ret = _retention(proj, tables, ret_gn_g[layer], batch, seq)
        c_col, c_row_t = _forget_cumsum(f_logits, bias, batch, seq)
        c_row = c_row_t[:, :N_HEADS, :].reshape(
            batch * N_HEADS, seq // FOX_BLOCK, 1, FOX_BLOCK)
        y_fox = _fox(proj, c_col, c_row, batch, seq, FOX_BLOCK)
        y_pool = _pool(proj, pool_w[layer], pool_scale[layer], batch, seq)
        x2 = _merge(x2, y_ret, y_fox, y_pool, proj, w_r, w_f, w_p, w_o,
                    final_g.reshape(1, d), layer, final_norm=(layer == depth - 1))
    return x2.reshape(batch, seq, d)
```

```python
import functools

import jax
import jax.numpy as jnp
from jax import lax
from jax.experimental import pallas as pl
from jax.experimental.pallas import tpu as pltpu

D_MODEL = 2048
HEAD_DIM = 128
N_HEADS = 8
WIDTH = D_MODEL // 2
POOL_WINDOWS = (2, 4, 8, 16)
POOL_GROUP_DIM = WIDTH // len(POOL_WINDOWS)
CHUNK = 128
ROPE_THETA = 10000.0
EPS = 1e-6
NEG_INF = -1e30
LOG2E = 1.4426950408889634
FOX_BLOCK = 512

LANES = 128
N_MAIN = 10 * WIDTH + 3 * D_MODEL
COL_RET = 0
COL_FOX = 4 * N_HEADS
COL_POOL_U = 8 * N_HEADS
COL_POOL_Z = 9 * N_HEADS
COL_GATES = 10 * N_HEADS

BF16 = jnp.bfloat16
F32 = jnp.float32
NT_DIMS = (((1,), (1,)), ((), ()))

VMEM_LIMIT = 56 * 1024 * 1024


def _params(sem, vmem=VMEM_LIMIT):
    return pltpu.CompilerParams(dimension_semantics=sem, vmem_limit_bytes=vmem)


def _inproj_kernel(x_ref, g_ref, w_ref, wf_ref, o_ref, f_ref, h_scr):
    @pl.when(pl.program_id(1) == 0)
    def _():
        x = x_ref[...]
        y = x * lax.rsqrt(jnp.mean(x * x, axis=-1, keepdims=True) + EPS)
        h = (y * g_ref[...]).astype(BF16)
        h_scr[...] = h
        f_ref[...] = lax.dot_general(h, wf_ref[...], NT_DIMS, preferred_element_type=F32)

    o_ref[...] = lax.dot_general(h_scr[...], w_ref[...], NT_DIMS, preferred_element_type=F32)


def _wprep_kernel(main_ref, tail_ref, out_ref, *, rows_before_gap):
    tr = out_ref.shape[0]
    gap = tail_ref.shape[0]
    i = pl.program_id(1)

    @pl.when(i * tr < rows_before_gap)
    def _():
        out_ref[...] = main_ref[...].astype(BF16)

    @pl.when(i * tr >= rows_before_gap)
    def _():
        out_ref[:tr - gap, :] = main_ref[gap:, :].astype(BF16)
        out_ref[tr - gap:, :] = tail_ref[...].astype(BF16)


def _prep_w_in(w_in_t, tr=512):
    depth, _, d = w_in_t.shape
    return pl.pallas_call(
        functools.partial(_wprep_kernel, rows_before_gap=8 * WIDTH),
        grid=(depth, N_MAIN // tr),
        in_specs=[
            pl.BlockSpec((None, tr, d), lambda l, i: (l, i, 0)),
            pl.BlockSpec((None, N_HEADS, d), lambda l, i: (l, (i + 1) * (tr // N_HEADS), 0)),
        ],
        out_specs=pl.BlockSpec((None, tr, d), lambda l, i: (l, i, 0)),
        out_shape=jax.ShapeDtypeStruct((depth, N_MAIN, d), BF16),
        compiler_params=_params(("parallel", "parallel")),
        name="w_in_prep",
    )(w_in_t, w_in_t)


def _inproj(x2, g, w_main_t, w_f_t, layer, tm=1024, tn=1024):
    m, d = x2.shape
    n = w_main_t.shape[1]
    return pl.pallas_call(
        _inproj_kernel,
        grid=(m // tm, n // tn),
        in_specs=[
            pl.BlockSpec((tm, d), lambda i, j: (i, 0)),
            pl.BlockSpec((1, d), lambda i, j: (0, 0)),
            pl.BlockSpec((None, tn, d), lambda i, j: (layer, j, 0)),
            pl.BlockSpec((None, LANES, d), lambda i, j: (layer, 0, 0)),
        ],
        out_specs=[
            pl.BlockSpec((tm, tn), lambda i, j: (i, j)),
            pl.BlockSpec((tm, LANES), lambda i, j: (i, 0)),
        ],
        out_shape=[
            jax.ShapeDtypeStruct((m, n), F32),
            jax.ShapeDtypeStruct((m, LANES), F32),
        ],
        scratch_shapes=[pltpu.VMEM((tm, d), BF16)],
        compiler_params=_params(("parallel", "arbitrary")),
        name="inproj",
    )(x2, g, w_main_t, w_f_t)


def _rotate(t, cos, sin_signed):
    return t * cos + pltpu.roll(t, HEAD_DIM // 2, 1) * sin_signed


def _retention_body(q_ref, k_ref, v_ref, z_ref, cos_ref, sin_ref, decay_ref,
                    xi_ref, zeta_ref, gch_ref, gn_ref, o_ref,
                    q_scr, k_scr, kz_scr, v_scr, inner_scr, state_scr, out_scr):
    decay = decay_ref[0]
    xi = xi_ref[0]
    zeta = zeta_ref[0]
    g_chunk = gch_ref[0, 0:1, :]
    n_chunks = q_ref.shape[0] // CHUNK
    chunk_rows = [pl.ds(c * CHUNK, CHUNK) for c in range(n_chunks)]
    nt = (((1,), (1,)), ((), ()))
    tn = (((0,), (0,)), ((), ()))

    cos = cos_ref[...]
    sin = sin_ref[...]
    q_scr[...] = _rotate(q_ref[...], cos, sin).astype(BF16)
    k = _rotate(k_ref[...], cos, sin) * (HEAD_DIM ** -0.5)
    k_scr[...] = k.astype(BF16)
    v_scr[...] = v_ref[...].astype(BF16)
    for c, rows in enumerate(chunk_rows):
        kz_scr[rows, :] = (k[c * CHUNK:(c + 1) * CHUNK] * zeta).astype(BF16)

    increments = [lax.dot_general(kz_scr[rows, :], v_scr[rows, :], tn, preferred_element_type=F32)
                  for rows in chunk_rows]
    for rows in chunk_rows:
        inner = lax.dot_general(q_scr[rows, :], k_scr[rows, :], nt,
                                preferred_element_type=F32) * decay
        inner_scr[rows, :] = inner.astype(BF16)

    state = jnp.zeros((HEAD_DIM, HEAD_DIM), F32)
    for rows, inc in zip(chunk_rows, increments):
        state_scr[rows, :] = state.astype(BF16)
        state = g_chunk * state + inc

    for rows in chunk_rows:
        out_scr[rows, :] = (
            jnp.dot(inner_scr[rows, :], v_scr[rows, :], preferred_element_type=F32)
            + jnp.dot(q_scr[rows, :], state_scr[rows, :], preferred_element_type=F32) * xi)

    out = out_scr[...]
    mu = jnp.mean(out, axis=-1, keepdims=True)
    dev = out - mu
    var = jnp.mean(dev * dev, axis=-1, keepdims=True)
    yn = dev * lax.rsqrt(var + EPS) * gn_ref[0]
    z = z_ref[...]
    o_ref[...] = (yn * (z * jax.nn.sigmoid(z))).astype(o_ref.dtype)


def _retention_tables(seq):
    half = HEAD_DIM // 2
    inv = ROPE_THETA ** (-jnp.arange(half, dtype=F32) / half)
    ang = jnp.arange(seq, dtype=F32)[:, None] * inv[None, :]
    cos = jnp.cos(ang)
    sin = jnp.sin(ang)
    cos2 = jnp.concatenate([cos, cos], axis=-1)
    sin2 = jnp.concatenate([-sin, sin], axis=-1)
    log_g = jnp.log1p(-jnp.exp2(-5.0 - jnp.arange(N_HEADS, dtype=F32)))
    idx = jnp.arange(CHUNK, dtype=F32)
    diff = idx[:, None] - idx[None, :]
    decay = jnp.where(diff >= 0, jnp.exp(log_g[:, None, None] * jnp.maximum(diff, 0.0)), 0.0)
    xi = jnp.exp(log_g[:, None] * (idx + 1.0))
    zeta = jnp.exp(log_g[:, None] * (CHUNK - 1.0 - idx))
    g_chunk = jnp.exp(log_g * CHUNK)
    bcast = lambda t: jnp.broadcast_to(t[:, :, None], (N_HEADS, CHUNK, HEAD_DIM))
    gch = jnp.broadcast_to(g_chunk[:, None, None], (N_HEADS, 8, HEAD_DIM))
    return cos2, sin2, decay, bcast(xi), bcast(zeta), gch


def _split3(x):
    hi = x.astype(BF16)
    r1 = x - hi.astype(F32)
    mid = r1.astype(BF16)
    lo = (r1 - mid.astype(F32)).astype(BF16)
    return hi, mid, lo


def _forget_cumsum_kernel(f_ref, b_ref, col_ref, row_ref):
    n_blocks = f_ref.shape[0] // CHUNK
    r = lax.broadcasted_iota(jnp.int32, (CHUNK, CHUNK), 0)
    c = lax.broadcasted_iota(jnp.int32, (CHUNK, CHUNK), 1)
    tri = jnp.where(r >= c, 1.0, 0.0).astype(BF16)
    carry = jnp.zeros((1, LANES), F32)
    for blk in range(n_blocks):
        rows = pl.ds(blk * CHUNK, CHUNK)
        x = f_ref[rows, :] + b_ref[...]
        ls = jnp.minimum(x, 0.0) - jnp.log1p(jnp.exp(-jnp.abs(x)))
        hi, mid, lo = _split3(ls)
        cs = (jnp.dot(tri, hi, preferred_element_type=F32)
              + jnp.dot(tri, mid, preferred_element_type=F32)
              + jnp.dot(tri, lo, preferred_element_type=F32)) + carry
        col_ref[rows, :] = cs
        row_ref[0, :, rows] = cs.T
        carry = cs[CHUNK - 1:CHUNK, :]


def _forget_cumsum(f_logits, bias, batch, seq):
    m = f_logits.shape[0]
    return pl.pallas_call(
        _forget_cumsum_kernel,
        grid=(batch,),
        in_specs=[
            pl.BlockSpec((seq, LANES), lambda b: (b, 0)),
            pl.BlockSpec((1, LANES), lambda b: (0, 0)),
        ],
        out_specs=[
            pl.BlockSpec((seq, LANES), lambda b: (b, 0)),
            pl.BlockSpec((1, LANES, seq), lambda b: (b, 0, 0)),
        ],
        out_shape=[
            jax.ShapeDtypeStruct((m, LANES), F32),
            jax.ShapeDtypeStruct((batch, LANES, seq), F32),
        ],
        compiler_params=_params(("parallel",)),
        name="forget_cumsum",
    )(f_logits, bias)


def _lane_groups(x):
    return [x[:, g * LANES:(g + 1) * LANES] for g in range(x.shape[1] // LANES)]


def _fox_body(q_ref, k_ref, v_ref, z_ref, ccol_ref, crow_ref, o_ref,
              m_scr, acc_scr, *, tq):
    h = pl.program_id(1)
    seq = q_ref.shape[0]
    c1 = (HEAD_DIM ** -0.5) * LOG2E
    ones = jnp.ones((tq, HEAD_DIM), BF16)

    qb = q_ref[...].astype(BF16)
    lane = lax.broadcasted_iota(jnp.int32, (seq, LANES), 1)
    cq2 = jnp.sum(jnp.where(lane == h, ccol_ref[...], 0.0), axis=-1, keepdims=True) * LOG2E
    causal = (lax.broadcasted_iota(jnp.int32, (tq, tq), 0)
              >= lax.broadcasted_iota(jnp.int32, (tq, tq), 1))

    for j in range(seq // tq):
        lo = j * tq
        live = slice(lo, seq)
        kj = k_ref[lo:lo + tq, :].astype(BF16)
        vj = jnp.concatenate([v_ref[lo:lo + tq, :].astype(BF16), ones], axis=-1)
        s = lax.dot_general(qb[live], kj, (((1,), (1,)), ((), ())), preferred_element_type=F32)
        a = s * c1 - crow_ref[0, j] * LOG2E
        diag = jnp.where(causal, a[:tq], NEG_INF)
        a = diag if lo + tq == seq else jnp.concatenate([diag, a[tq:]], axis=0)
        groups = _lane_groups(a)
        m_blk = jnp.max(functools.reduce(jnp.maximum, groups), axis=-1, keepdims=True) + cq2[live]
        if j == 0:
            m_new = jnp.broadcast_to(m_blk, (seq, LANES))
        else:
            m_prev = m_scr[live, :]
            m_new = jnp.maximum(m_prev, m_blk)
            alpha = jnp.exp2(m_prev - m_new)
        shift = m_new - cq2[live]
        p = jnp.concatenate([jnp.exp2(g - shift).astype(BF16) for g in groups], axis=-1)
        pv = jnp.dot(p, vj, preferred_element_type=F32)
        m_scr[live, :] = m_new
        if j == 0:
            acc_scr[...] = pv
        else:
            acc_scr[live, :] = jnp.concatenate([alpha, alpha], axis=-1) * acc_scr[live, :] + pv

    z = z_ref[...]
    o_ref[...] = ((acc_scr[:, :HEAD_DIM] / acc_scr[:, HEAD_DIM:])
                  * (z * jax.nn.sigmoid(z))).astype(o_ref.dtype)


N_RET_IN, N_FOX_IN, N_RET_SCR = 11, 6, 7


def _mixers_kernel(*refs, tq):
    ret_in, refs = refs[:N_RET_IN], refs[N_RET_IN:]
    fox_in, refs = refs[:N_FOX_IN], refs[N_FOX_IN:]
    (y_ret_ref, y_fox_ref), refs = refs[:2], refs[2:]
    ret_scr, fox_scr = refs[:N_RET_SCR], refs[N_RET_SCR:]
    _fox_body(*fox_in, y_fox_ref, *fox_scr, tq=tq)
    _retention_body(*ret_in, y_ret_ref, *ret_scr)


def _mixers(proj, tables, gn_g, c_col, c_row, batch, seq, tq):
    cos, sin, decay, xi, zeta, gch = tables
    m = proj.shape[0]
    head_block = lambda off: pl.BlockSpec((seq, HEAD_DIM), lambda b, h: (b, off + h))
    per_head = lambda shape: pl.BlockSpec((1,) + shape, lambda b, h: (h, 0, 0))
    table = pl.BlockSpec((seq, HEAD_DIM), lambda b, h: (0, 0))
    y_spec = pl.BlockSpec((seq, HEAD_DIM), lambda b, h: (b, h))
    y_shape = jax.ShapeDtypeStruct((m, WIDTH), BF16)
    return pl.pallas_call(
        functools.partial(_mixers_kernel, tq=tq),
        grid=(batch, N_HEADS),
        in_specs=[
            head_block(COL_RET), head_block(COL_RET + N_HEADS),
            head_block(COL_RET + 2 * N_HEADS), head_block(COL_RET + 3 * N_HEADS),
            table, table,
            per_head((CHUNK, CHUNK)), per_head((CHUNK, HEAD_DIM)),
            per_head((CHUNK, HEAD_DIM)), per_head((8, HEAD_DIM)), per_head((1, HEAD_DIM)),
            head_block(COL_FOX), head_block(COL_FOX + N_HEADS),
            head_block(COL_FOX + 2 * N_HEADS), head_block(COL_FOX + 3 * N_HEADS),
            pl.BlockSpec((seq, LANES), lambda b, h: (b, 0)),
            pl.BlockSpec((1, seq // tq, 1, tq), lambda b, h: (b * N_HEADS + h, 0, 0, 0)),
        ],
        out_specs=[y_spec, y_spec],
        out_shape=[y_shape, y_shape],
        scratch_shapes=([pltpu.VMEM((seq, HEAD_DIM), BF16)] * 6 + [pltpu.VMEM((seq, HEAD_DIM), F32)]
                        + [pltpu.VMEM((seq, LANES), F32), pltpu.VMEM((seq, 2 * HEAD_DIM), F32)]),
        compiler_params=_params(("parallel", "arbitrary")),
        name="mixers",
    )(proj, proj, proj, proj, cos, sin, decay, xi, zeta, gch, gn_g.reshape(N_HEADS, 1, HEAD_DIM),
      proj, proj, proj, proj, c_col, c_row)


POOL_HALO = 16


def _pool_tile(u_ref, halo_ref, z_ref, w_ref, s_ref, tile_in_seq):
    tm = u_ref.shape[0]
    halo = jnp.where(tile_in_seq == 0, 0.0, halo_ref[...])
    t1 = (tile_in_seq * tm + lax.broadcasted_iota(jnp.int32, (tm, LANES), 0) + 1).astype(F32)
    mixed = []
    for gi, w in enumerate(POOL_WINDOWS):
        cols = slice(gi * POOL_GROUP_DIM, (gi + 1) * POOL_GROUP_DIM)
        u = u_ref[:, cols]
        total, k = jnp.concatenate([halo[:, cols], u], axis=0), 1
        while k < w:
            total = total + pltpu.roll(total, k, 0)
            k *= 2
        inv = 1.0 / jnp.minimum(t1, float(w))
        mean = jnp.concatenate([x * inv for x in _lane_groups(total[POOL_HALO:])], axis=-1)
        mixed.append(jnp.dot((mean - u).astype(BF16), w_ref[gi], preferred_element_type=F32)
                     * s_ref[gi])
    z = z_ref[...]
    return (jnp.concatenate(mixed, axis=-1) * (z * jax.nn.sigmoid(z))).astype(BF16)


def _merge_kernel(x_ref, yr_ref, yf_ref, u_ref, halo_ref, pz_ref, ga_ref, gb_ref, gc_ref,
                  pw_ref, ps_ref, wr_ref, wf_ref, wp_ref, wo_ref, fg_ref, o_ref,
                  *, tiles_per_seq, final_norm):
    y_pool = _pool_tile(u_ref, halo_ref, pz_ref, pw_ref, ps_ref,
                        pl.program_id(0) % tiles_per_seq)
    merged = (jax.nn.sigmoid(ga_ref[...]) * jnp.dot(yr_ref[...], wr_ref[...], preferred_element_type=F32)
              + jax.nn.sigmoid(gb_ref[...]) * jnp.dot(yf_ref[...], wf_ref[...], preferred_element_type=F32)
              + jax.nn.sigmoid(gc_ref[...]) * jnp.dot(y_pool, wp_ref[...], preferred_element_type=F32))
    x = x_ref[...] + jnp.dot(merged.astype(BF16), wo_ref[...], preferred_element_type=F32)
    if final_norm:
        x = x * lax.rsqrt(jnp.mean(x * x, axis=-1, keepdims=True) + EPS) * fg_ref[...]
    o_ref[...] = x


def _merge(x2, y_ret, y_fox, proj, pool_w, pool_scale, w_r, w_f, w_p, w_o, final_g,
           layer, seq, final_norm, tm=256):
    m, d = x2.shape
    n_groups = len(POOL_WINDOWS)
    halo_per_tile = tm // POOL_HALO
    row = lambda width: pl.BlockSpec((tm, width), lambda i: (i, 0))
    col = lambda width, off: pl.BlockSpec((tm, width), lambda i: (i, off * LANES // width))
    once = lambda shape, index: pl.BlockSpec(shape, index, pipeline_mode=pl.Buffered(1))
    weight = lambda rows: once((None, rows, d), lambda i: (layer, 0, 0))
    halo = pl.BlockSpec((POOL_HALO, WIDTH),
                        lambda i: (jnp.maximum(i * halo_per_tile - 1, 0), COL_POOL_U * LANES // WIDTH))
    return pl.pallas_call(
        functools.partial(_merge_kernel, tiles_per_seq=seq // tm, final_norm=final_norm),
        grid=(m // tm,),
        in_specs=[
            row(d), row(WIDTH), row(WIDTH),
            col(WIDTH, COL_POOL_U), halo, col(WIDTH, COL_POOL_Z),
            col(d, COL_GATES), col(d, COL_GATES + d // LANES), col(d, COL_GATES + 2 * d // LANES),
            once((None, n_groups, POOL_GROUP_DIM, POOL_GROUP_DIM), lambda i: (layer, 0, 0, 0)),
            once((None, n_groups, 1, POOL_GROUP_DIM), lambda i: (layer, 0, 0, 0)),
            weight(WIDTH), weight(WIDTH), weight(WIDTH), weight(d),
            once((1, d), lambda i: (0, 0)),
        ],
        out_specs=row(d),
        out_shape=jax.ShapeDtypeStruct((m, d), F32),
        compiler_params=_params(("parallel",)),
        name="merge_out",
    )(x2, y_ret, y_fox, proj, proj, proj, proj, proj, proj,
      pool_w, pool_scale, w_r, w_f, w_p, w_o, final_g)


def kernel(x, norm_g, w_in, ret_gn_g, fox_b_f, pool_w, pool_scale,
           w_ret_branch, w_fox_branch, w_pool_branch, w_out, final_g):
    batch, seq, d = x.shape
    depth = w_in.shape[0]
    tables = _retention_tables(seq)
    n_before = 8 * WIDTH
    w_in_t = jnp.swapaxes(w_in, 1, 2)
    w_main = _prep_w_in(w_in_t)
    w_forget = jnp.pad(w_in_t[:, n_before:n_before + N_HEADS, :],
                       ((0, 0), (0, LANES - N_HEADS), (0, 0))).astype(BF16)
    w_r, w_f, w_p, w_o = (t.astype(BF16) for t in (w_ret_branch, w_fox_branch, w_pool_branch, w_out))
    pool_w_b = pool_w.astype(BF16)
    pool_scale_r = pool_scale.reshape(depth, len(POOL_WINDOWS), 1, POOL_GROUP_DIM)
    x2 = x.reshape(batch * seq, d)
    for layer in range(depth):
        bias = jnp.pad(fox_b_f[layer], (0, LANES - N_HEADS)).reshape(1, LANES)

        proj, f_logits = _inproj(x2, norm_g[layer].reshape(1, d), w_main, w_forget, layer)
        c_col, c_row_t = _forget_cumsum(f_logits, bias, batch, seq)
        c_row = c_row_t[:, :N_HEADS, :].reshape(
            batch * N_HEADS, seq // FOX_BLOCK, 1, FOX_BLOCK)
        y_ret, y_fox = _mixers(proj, tables, ret_gn_g[layer], c_col, c_row, batch, seq, FOX_BLOCK)
        x2 = _merge(x2, y_ret, y_fox, proj, pool_w_b, pool_scale_r, w_r, w_f, w_p, w_o,
                    final_g.reshape(1, d), layer, seq, final_norm=(layer == depth - 1))
    return x2.reshape(batch, seq, d)
```

```python
import functools

import jax
import jax.numpy as jnp
from jax import lax
from jax.experimental import pallas as pl
from jax.experimental.pallas import tpu as pltpu

D_MODEL = 2048
HEAD_DIM = 128
N_HEADS = 8
WIDTH = D_MODEL // 2
POOL_WINDOWS = (2, 4, 8, 16)
POOL_GROUP_DIM = WIDTH // len(POOL_WINDOWS)
CHUNK = 128
ROPE_THETA = 10000.0
EPS = 1e-6
NEG_INF = -1e30
LOG2E = 1.4426950408889634
FOX_BLOCK = 512

LANES = 128
N_MAIN = 10 * WIDTH + 3 * D_MODEL
COL_RET = 0
COL_FOX = 4 * N_HEADS
COL_POOL_U = 8 * N_HEADS
COL_POOL_Z = 9 * N_HEADS
COL_GATES = 10 * N_HEADS

BF16 = jnp.bfloat16
F32 = jnp.float32
NT_DIMS = (((1,), (1,)), ((), ()))

VMEM_LIMIT = 56 * 1024 * 1024


def _params(sem, vmem=VMEM_LIMIT):
    return pltpu.CompilerParams(dimension_semantics=sem, vmem_limit_bytes=vmem)


def _inproj_kernel(x_ref, g_ref, w_ref, wf_ref, o_ref, f_ref, h_scr):
    @pl.when(pl.program_id(1) == 0)
    def _():
        x = x_ref[...]
        y = x * lax.rsqrt(jnp.mean(x * x, axis=-1, keepdims=True) + EPS)
        h = (y * g_ref[...]).astype(BF16)
        h_scr[...] = h
        f_ref[...] = lax.dot_general(h, wf_ref[...], NT_DIMS, preferred_element_type=F32)

    o_ref[...] = lax.dot_general(h_scr[...], w_ref[...], NT_DIMS, preferred_element_type=F32)


def _wprep_kernel(main_ref, tail_ref, out_ref, *, rows_before_gap):
    tr = out_ref.shape[0]
    gap = tail_ref.shape[0]
    i = pl.program_id(1)

    @pl.when(i * tr < rows_before_gap)
    def _():
        out_ref[...] = main_ref[...].astype(BF16)

    @pl.when(i * tr >= rows_before_gap)
    def _():
        out_ref[:tr - gap, :] = main_ref[gap:, :].astype(BF16)
        out_ref[tr - gap:, :] = tail_ref[...].astype(BF16)


def _prep_w_in(w_in_t, tr=512):
    depth, _, d = w_in_t.shape
    return pl.pallas_call(
        functools.partial(_wprep_kernel, rows_before_gap=8 * WIDTH),
        grid=(depth, N_MAIN // tr),
        in_specs=[
            pl.BlockSpec((None, tr, d), lambda l, i: (l, i, 0)),
            pl.BlockSpec((None, N_HEADS, d), lambda l, i: (l, (i + 1) * (tr // N_HEADS), 0)),
        ],
        out_specs=pl.BlockSpec((None, tr, d), lambda l, i: (l, i, 0)),
        out_shape=jax.ShapeDtypeStruct((depth, N_MAIN, d), BF16),
        compiler_params=_params(("parallel", "parallel")),
        name="w_in_prep",
    )(w_in_t, w_in_t)


def _inproj(x2, g, w_main_t, w_f_t, layer, tm=1024, tn=1024):
    m, d = x2.shape
    n = w_main_t.shape[1]
    return pl.pallas_call(
        _inproj_kernel,
        grid=(m // tm, n // tn),
        in_specs=[
            pl.BlockSpec((tm, d), lambda i, j: (i, 0)),
            pl.BlockSpec((1, d), lambda i, j: (0, 0)),
            pl.BlockSpec((None, tn, d), lambda i, j: (layer, j, 0)),
            pl.BlockSpec((None, LANES, d), lambda i, j: (layer, 0, 0)),
        ],
        out_specs=[
            pl.BlockSpec((tm, tn), lambda i, j: (i, j)),
            pl.BlockSpec((tm, LANES), lambda i, j: (i, 0)),
        ],
        out_shape=[
            jax.ShapeDtypeStruct((m, n), F32),
            jax.ShapeDtypeStruct((m, LANES), F32),
        ],
        scratch_shapes=[pltpu.VMEM((tm, d), BF16)],
        compiler_params=_params(("parallel", "arbitrary")),
        name="inproj",
    )(x2, g, w_main_t, w_f_t)


def _rotate(t, cos, sin_signed):
    return t * cos + pltpu.roll(t, HEAD_DIM // 2, 1) * sin_signed


def _retention_body(q_ref, k_ref, v_ref, z_ref, cos_ref, sin_ref, decay_ref,
                    xi_ref, zeta_ref, gch_ref, gn_ref, o_ref,
                    q_scr, k_scr, kz_scr, v_scr, inner_scr, state_scr, out_scr):
    decay = decay_ref[0]
    xi = xi_ref[0]
    zeta = zeta_ref[0]
    g_chunk = gch_ref[0, 0:1, :]
    n_chunks = q_ref.shape[0] // CHUNK
    chunk_rows = [pl.ds(c * CHUNK, CHUNK) for c in range(n_chunks)]
    nt = (((1,), (1,)), ((), ()))
    tn = (((0,), (0,)), ((), ()))

    cos = cos_ref[...]
    sin = sin_ref[...]
    q_scr[...] = _rotate(q_ref[...], cos, sin).astype(BF16)
    k = _rotate(k_ref[...], cos, sin) * (HEAD_DIM ** -0.5)
    k_scr[...] = k.astype(BF16)
    v_scr[...] = v_ref[...].astype(BF16)
    for c, rows in enumerate(chunk_rows):
        kz_scr[rows, :] = (k[c * CHUNK:(c + 1) * CHUNK] * zeta).astype(BF16)

    increments = [lax.dot_general(kz_scr[rows, :], v_scr[rows, :], tn, preferred_element_type=F32)
                  for rows in chunk_rows]
    for rows in chunk_rows:
        inner = lax.dot_general(q_scr[rows, :], k_scr[rows, :], nt,
                                preferred_element_type=F32) * decay
        inner_scr[rows, :] = inner.astype(BF16)

    state = jnp.zeros((HEAD_DIM, HEAD_DIM), F32)
    for rows, inc in zip(chunk_rows, increments):
        state_scr[rows, :] = state.astype(BF16)
        state = g_chunk * state + inc

    for rows in chunk_rows:
        out_scr[rows, :] = (
            jnp.dot(inner_scr[rows, :], v_scr[rows, :], preferred_element_type=F32)
            + jnp.dot(q_scr[rows, :], state_scr[rows, :], preferred_element_type=F32) * xi)

    out = out_scr[...]
    mu = jnp.mean(out, axis=-1, keepdims=True)
    dev = out - mu
    var = jnp.mean(dev * dev, axis=-1, keepdims=True)
    yn = dev * lax.rsqrt(var + EPS) * gn_ref[0]
    z = z_ref[...]
    o_ref[...] = (yn * (z * jax.nn.sigmoid(z))).astype(o_ref.dtype)


def _retention_tables(seq):
    half = HEAD_DIM // 2
    inv = ROPE_THETA ** (-jnp.arange(half, dtype=F32) / half)
    ang = jnp.arange(seq, dtype=F32)[:, None] * inv[None, :]
    cos = jnp.cos(ang)
    sin = jnp.sin(ang)
    cos2 = jnp.concatenate([cos, cos], axis=-1)
    sin2 = jnp.concatenate([-sin, sin], axis=-1)
    log_g = jnp.log1p(-jnp.exp2(-5.0 - jnp.arange(N_HEADS, dtype=F32)))
    idx = jnp.arange(CHUNK, dtype=F32)
    diff = idx[:, None] - idx[None, :]
    decay = jnp.where(diff >= 0, jnp.exp(log_g[:, None, None] * jnp.maximum(diff, 0.0)), 0.0)
    xi = jnp.exp(log_g[:, None] * (idx + 1.0))
    zeta = jnp.exp(log_g[:, None] * (CHUNK - 1.0 - idx))
    g_chunk = jnp.exp(log_g * CHUNK)
    bcast = lambda t: jnp.broadcast_to(t[:, :, None], (N_HEADS, CHUNK, HEAD_DIM))
    gch = jnp.broadcast_to(g_chunk[:, None, None], (N_HEADS, 8, HEAD_DIM))
    return cos2, sin2, decay, bcast(xi), bcast(zeta), gch


def _split3(x):
    hi = x.astype(BF16)
    r1 = x - hi.astype(F32)
    mid = r1.astype(BF16)
    lo = (r1 - mid.astype(F32)).astype(BF16)
    return hi, mid, lo


def _forget_cumsum_kernel(f_ref, b_ref, col_ref, row_ref):
    n_blocks = f_ref.shape[0] // CHUNK
    r = lax.broadcasted_iota(jnp.int32, (CHUNK, CHUNK), 0)
    c = lax.broadcasted_iota(jnp.int32, (CHUNK, CHUNK), 1)
    tri = jnp.where(r >= c, 1.0, 0.0).astype(BF16)
    carry = jnp.zeros((1, LANES), F32)
    for blk in range(n_blocks):
        rows = pl.ds(blk * CHUNK, CHUNK)
        x = f_ref[rows, :] + b_ref[...]
        ls = jnp.minimum(x, 0.0) - jnp.log1p(jnp.exp(-jnp.abs(x)))
        hi, mid, lo = _split3(ls)
        cs = (jnp.dot(tri, hi, preferred_element_type=F32)
              + jnp.dot(tri, mid, preferred_element_type=F32)
              + jnp.dot(tri, lo, preferred_element_type=F32)) + carry
        col_ref[rows, :] = cs
        row_ref[0, :, rows] = cs.T
        carry = cs[CHUNK - 1:CHUNK, :]


def _forget_cumsum(f_logits, bias, batch, seq):
    m = f_logits.shape[0]
    return pl.pallas_call(
        _forget_cumsum_kernel,
        grid=(batch,),
        in_specs=[
            pl.BlockSpec((seq, LANES), lambda b: (b, 0)),
            pl.BlockSpec((1, LANES), lambda b: (0, 0)),
        ],
        out_specs=[
            pl.BlockSpec((seq, LANES), lambda b: (b, 0)),
            pl.BlockSpec((1, LANES, seq), lambda b: (b, 0, 0)),
        ],
        out_shape=[
            jax.ShapeDtypeStruct((m, LANES), F32),
            jax.ShapeDtypeStruct((batch, LANES, seq), F32),
        ],
        compiler_params=_params(("parallel",)),
        name="forget_cumsum",
    )(f_logits, bias)


def _lane_groups(x):
    return [x[:, g * LANES:(g + 1) * LANES] for g in range(x.shape[1] // LANES)]


def _fox_body(q_ref, k_ref, v_ref, z_ref, ccol_ref, crow_ref, o_ref,
              m_scr, acc_scr, *, tq):
    h = pl.program_id(1)
    seq = q_ref.shape[0]
    c1 = (HEAD_DIM ** -0.5) * LOG2E
    ones = jnp.ones((tq, HEAD_DIM), BF16)

    qb = q_ref[...].astype(BF16)
    lane = lax.broadcasted_iota(jnp.int32, (seq, LANES), 1)
    cq2 = jnp.sum(jnp.where(lane == h, ccol_ref[...], 0.0), axis=-1, keepdims=True) * LOG2E
    causal = (lax.broadcasted_iota(jnp.int32, (tq, tq), 0)
              >= lax.broadcasted_iota(jnp.int32, (tq, tq), 1))

    for j in range(seq // tq):
        lo = j * tq
        live = slice(lo, seq)
        kj = k_ref[lo:lo + tq, :].astype(BF16)
        vj = jnp.concatenate([v_ref[lo:lo + tq, :].astype(BF16), ones], axis=-1)
        s = lax.dot_general(qb[live], kj, (((1,), (1,)), ((), ())), preferred_element_type=F32)
        a = s * c1 - crow_ref[0, j] * LOG2E
        diag = jnp.where(causal, a[:tq], NEG_INF)
        a = diag if lo + tq == seq else jnp.concatenate([diag, a[tq:]], axis=0)
        groups = _lane_groups(a)
        m_blk = jnp.max(functools.reduce(jnp.maximum, groups), axis=-1, keepdims=True) + cq2[live]
        if j == 0:
            m_new = jnp.broadcast_to(m_blk, (seq, LANES))
        else:
            m_prev = m_scr[live, :]
            m_new = jnp.maximum(m_prev, m_blk)
            alpha = jnp.exp2(m_prev - m_new)
        shift = m_new - cq2[live]
        p = jnp.concatenate([jnp.exp2(g - shift).astype(BF16) for g in groups], axis=-1)
        pv = jnp.dot(p, vj, preferred_element_type=F32)
        m_scr[live, :] = m_new
        if j == 0:
            acc_scr[...] = pv
        else:
            acc_scr[live, :] = jnp.concatenate([alpha, alpha], axis=-1) * acc_scr[live, :] + pv

    z = z_ref[...]
    o_ref[...] = ((acc_scr[:, :HEAD_DIM] / acc_scr[:, HEAD_DIM:])
                  * (z * jax.nn.sigmoid(z))).astype(o_ref.dtype)


N_RET_IN, N_FOX_IN, N_RET_SCR = 11, 6, 7


def _mixers_kernel(*refs, tq):
    ret_in, refs = refs[:N_RET_IN], refs[N_RET_IN:]
    fox_in, refs = refs[:N_FOX_IN], refs[N_FOX_IN:]
    (y_ret_ref, y_fox_ref), refs = refs[:2], refs[2:]
    ret_scr, fox_scr = refs[:N_RET_SCR], refs[N_RET_SCR:]
    _fox_body(*fox_in, y_fox_ref, *fox_scr, tq=tq)
    _retention_body(*ret_in, y_ret_ref, *ret_scr)


def _mixers(proj, tables, gn_g, c_col, c_row, batch, seq, tq):
    cos, sin, decay, xi, zeta, gch = tables
    m = proj.shape[0]
    head_block = lambda off: pl.BlockSpec((seq, HEAD_DIM), lambda b, h: (b, off + h))
    per_head = lambda shape: pl.BlockSpec((1,) + shape, lambda b, h: (h, 0, 0))
    table = pl.BlockSpec((seq, HEAD_DIM), lambda b, h: (0, 0))
    y_spec = pl.BlockSpec((seq, HEAD_DIM), lambda b, h: (b, h))
    y_shape = jax.ShapeDtypeStruct((m, WIDTH), BF16)
    return pl.pallas_call(
        functools.partial(_mixers_kernel, tq=tq),
        grid=(batch, N_HEADS),
        in_specs=[
            head_block(COL_RET), head_block(COL_RET + N_HEADS),
            head_block(COL_RET + 2 * N_HEADS), head_block(COL_RET + 3 * N_HEADS),
            table, table,
            per_head((CHUNK, CHUNK)), per_head((CHUNK, HEAD_DIM)),
            per_head((CHUNK, HEAD_DIM)), per_head((8, HEAD_DIM)), per_head((1, HEAD_DIM)),
            head_block(COL_FOX), head_block(COL_FOX + N_HEADS),
            head_block(COL_FOX + 2 * N_HEADS), head_block(COL_FOX + 3 * N_HEADS),
            pl.BlockSpec((seq, LANES), lambda b, h: (b, 0)),
            pl.BlockSpec((1, seq // tq, 1, tq), lambda b, h: (b * N_HEADS + h, 0, 0, 0)),
        ],
        out_specs=[y_spec, y_spec],
        out_shape=[y_shape, y_shape],
        scratch_shapes=([pltpu.VMEM((seq, HEAD_DIM), BF16)] * 6 + [pltpu.VMEM((seq, HEAD_DIM), F32)]
                        + [pltpu.VMEM((seq, LANES), F32), pltpu.VMEM((seq, 2 * HEAD_DIM), F32)]),
        compiler_params=_params(("parallel", "arbitrary")),
        name="mixers",
    )(proj, proj, proj, proj, cos, sin, decay, xi, zeta, gch, gn_g.reshape(N_HEADS, 1, HEAD_DIM),
      proj, proj, proj, proj, c_col, c_row)


POOL_HALO = 16


def _pool_group(gi, u_ref, halo_ref, z_ref, w_ref, s_ref, tile_in_seq):
    tm = u_ref.shape[0]
    w = POOL_WINDOWS[gi]
    cols = slice(gi * POOL_GROUP_DIM, (gi + 1) * POOL_GROUP_DIM)
    halo = jnp.where(tile_in_seq == 0, 0.0, halo_ref[:, cols])
    t1 = (tile_in_seq * tm + lax.broadcasted_iota(jnp.int32, (tm, LANES), 0) + 1).astype(F32)
    u = u_ref[:, cols]
    total, k = jnp.concatenate([halo, u], axis=0), 1
    while k < w:
        total = total + pltpu.roll(total, k, 0)
        k *= 2
    inv = 1.0 / jnp.minimum(t1, float(w))
    mean = jnp.concatenate([x * inv for x in _lane_groups(total[POOL_HALO:])], axis=-1)
    mixed = jnp.dot((mean - u).astype(BF16), w_ref[gi], preferred_element_type=F32) * s_ref[gi]
    z = z_ref[:, cols]
    return (mixed * (z * jax.nn.sigmoid(z))).astype(BF16)


def _merge_kernel(x_ref, yr_ref, yf_ref, u_ref, halo_ref, pz_ref, ga_ref, gb_ref, gc_ref,
                  pw_ref, ps_ref, wr_ref, wf_ref, wp_ref, wo_ref, fg_ref, o_ref,
                  *, tiles_per_seq, final_norm):
    tile_in_seq = pl.program_id(0) % tiles_per_seq
    quarter = o_ref.shape[1] // 4
    pool = functools.partial(_pool_group, u_ref=u_ref, halo_ref=halo_ref, z_ref=pz_ref,
                             w_ref=pw_ref, s_ref=ps_ref, tile_in_seq=tile_in_seq)

    yr = yr_ref[...]
    br, yp = [], []
    for part in range(4):
        cols = slice(part * quarter, (part + 1) * quarter)
        br.append(jnp.dot(yr, wr_ref[:, cols], preferred_element_type=F32))
        yp.append(pool(part))
    merged = (jax.nn.sigmoid(ga_ref[...]) * jnp.concatenate(br, axis=-1)
              + jax.nn.sigmoid(gb_ref[...]) * jnp.dot(yf_ref[...], wf_ref[...], preferred_element_type=F32)
              + jax.nn.sigmoid(gc_ref[...]) * jnp.dot(jnp.concatenate(yp, axis=-1), wp_ref[...],
                                                      preferred_element_type=F32))
    x = x_ref[...] + jnp.dot(merged.astype(BF16), wo_ref[...], preferred_element_type=F32)
    if final_norm:
        x = x * lax.rsqrt(jnp.mean(x * x, axis=-1, keepdims=True) + EPS) * fg_ref[...]
    o_ref[...] = x


def _merge(x2, y_ret, y_fox, proj, pool_w, pool_scale, w_r, w_f, w_p, w_o, final_g,
           layer, seq, final_norm, tm=256):
    m, d = x2.shape
    n_groups = len(POOL_WINDOWS)
    halo_per_tile = tm // POOL_HALO
    row = lambda width: pl.BlockSpec((tm, width), lambda i: (i, 0))
    col = lambda width, off: pl.BlockSpec((tm, width), lambda i: (i, off * LANES // width))
    once = lambda shape, index: pl.BlockSpec(shape, index, pipeline_mode=pl.Buffered(1))
    weight = lambda rows: once((None, rows, d), lambda i: (layer, 0, 0))
    halo = pl.BlockSpec((POOL_HALO, WIDTH),
                        lambda i: (jnp.maximum(i * halo_per_tile - 1, 0), COL_POOL_U * LANES // WIDTH))
    return pl.pallas_call(
        functools.partial(_merge_kernel, tiles_per_seq=seq // tm, final_norm=final_norm),
        grid=(m // tm,),
        in_specs=[
            row(d), row(WIDTH), row(WIDTH),
            col(WIDTH, COL_POOL_U), halo, col(WIDTH, COL_POOL_Z),
            col(d, COL_GATES), col(d, COL_GATES + d // LANES), col(d, COL_GATES + 2 * d // LANES),
            once((None, n_groups, POOL_GROUP_DIM, POOL_GROUP_DIM), lambda i: (layer, 0, 0, 0)),
            once((None, n_groups, 1, POOL_GROUP_DIM), lambda i: (layer, 0, 0, 0)),
            weight(WIDTH), weight(WIDTH), weight(WIDTH), weight(d),
            once((1, d), lambda i: (0, 0)),
        ],
        out_specs=row(d),
        out_shape=jax.ShapeDtypeStruct((m, d), F32),
        compiler_params=_params(("parallel",)),
        name="merge_out",
    )(x2, y_ret, y_fox, proj, proj, proj, proj, proj, proj,
      pool_w, pool_scale, w_r, w_f, w_p, w_o, final_g)


def kernel(x, norm_g, w_in, ret_gn_g, fox_b_f, pool_w, pool_scale,
           w_ret_branch, w_fox_branch, w_pool_branch, w_out, final_g):
    batch, seq, d = x.shape
    depth = w_in.shape[0]
    tables = _retention_tables(seq)
    n_before = 8 * WIDTH
    w_in_t = jnp.swapaxes(w_in, 1, 2)
    w_main = _prep_w_in(w_in_t)
    w_forget = jnp.pad(w_in_t[:, n_before:n_before + N_HEADS, :],
                       ((0, 0), (0, LANES - N_HEADS), (0, 0))).astype(BF16)
    w_r, w_f, w_p, w_o = (t.astype(BF16) for t in (w_ret_branch, w_fox_branch, w_pool_branch, w_out))
    pool_w_b = pool_w.astype(BF16)
    pool_scale_r = pool_scale.reshape(depth, len(POOL_WINDOWS), 1, POOL_GROUP_DIM)
    x2 = x.reshape(batch * seq, d)
    for layer in range(depth):
        bias = jnp.pad(fox_b_f[layer], (0, LANES - N_HEADS)).reshape(1, LANES)

        proj, f_logits = _inproj(x2, norm_g[layer].reshape(1, d), w_main, w_forget, layer)
        c_col, c_row_t = _forget_cumsum(f_logits, bias, batch, seq)
        c_row = c_row_t[:, :N_HEADS, :].reshape(
            batch * N_HEADS, seq // FOX_BLOCK, 1, FOX_BLOCK)
        y_ret, y_fox = _mixers(proj, tables, ret_gn_g[layer], c_col, c_row, batch, seq, FOX_BLOCK)
        x2 = _merge(x2, y_ret, y_fox, proj, pool_w_b, pool_scale_r, w_r, w_f, w_p, w_o,
                    final_g.reshape(1, d), layer, seq, final_norm=(layer == depth - 1))
    return x2.reshape(batch, seq, d)
```

```python
import functools

import jax
import jax.numpy as jnp
from jax import lax
from jax.experimental import pallas as pl
from jax.experimental.pallas import tpu as pltpu

D_MODEL = 2048
HEAD_DIM = 128
N_HEADS = 8
WIDTH = D_MODEL // 2
POOL_WINDOWS = (2, 4, 8, 16)
POOL_GROUP_DIM = WIDTH // len(POOL_WINDOWS)
CHUNK = 128
ROPE_THETA = 10000.0
EPS = 1e-6
NEG_INF = -1e30
LOG2E = 1.4426950408889634
FOX_BLOCK = 512

LANES = 128
N_MAIN = 10 * WIDTH + 3 * D_MODEL
COL_RET = 0
COL_FOX = 4 * N_HEADS
COL_POOL_U = 8 * N_HEADS
COL_POOL_Z = 9 * N_HEADS
COL_GATES = 10 * N_HEADS

BF16 = jnp.bfloat16
F32 = jnp.float32
NT_DIMS = (((1,), (1,)), ((), ()))

VMEM_LIMIT = 56 * 1024 * 1024


def _params(sem, vmem=VMEM_LIMIT):
    return pltpu.CompilerParams(dimension_semantics=sem, vmem_limit_bytes=vmem)


def _inproj_kernel(x_hbm, g_ref, w_ref, wf_ref, o_ref, f_ref, x_buf, h_scr, x_sem):
    i, j = pl.program_id(0), pl.program_id(1)
    tm = x_buf.shape[0]

    def x_copy(tile):
        return pltpu.make_async_copy(x_hbm.at[pl.ds(tile * tm, tm), :], x_buf, x_sem)

    @pl.when((i == 0) & (j == 0))
    def _():
        x_copy(0).start()

    @pl.when(j == 0)
    def _():
        x_copy(i).wait()
        x = x_buf[...]
        y = x * lax.rsqrt(jnp.mean(x * x, axis=-1, keepdims=True) + EPS)
        h = (y * g_ref[...]).astype(BF16)
        h_scr[...] = h
        f_ref[...] = lax.dot_general(h, wf_ref[...], NT_DIMS, preferred_element_type=F32)

    @pl.when((j == 1) & (i + 1 < pl.num_programs(0)))
    def _():
        x_copy(i + 1).start()

    o_ref[...] = lax.dot_general(h_scr[...], w_ref[...], NT_DIMS, preferred_element_type=F32)


def _wprep_kernel(main_ref, tail_ref, out_ref, *, rows_before_gap):
    tr = out_ref.shape[0]
    gap = tail_ref.shape[0]
    i = pl.program_id(1)

    @pl.when(i * tr < rows_before_gap)
    def _():
        out_ref[...] = main_ref[...].astype(BF16)

    @pl.when(i * tr >= rows_before_gap)
    def _():
        out_ref[:tr - gap, :] = main_ref[gap:, :].astype(BF16)
        out_ref[tr - gap:, :] = tail_ref[...].astype(BF16)


def _prep_w_in(w_in_t, tr=512):
    depth, _, d = w_in_t.shape
    return pl.pallas_call(
        functools.partial(_wprep_kernel, rows_before_gap=8 * WIDTH),
        grid=(depth, N_MAIN // tr),
        in_specs=[
            pl.BlockSpec((None, tr, d), lambda l, i: (l, i, 0)),
            pl.BlockSpec((None, N_HEADS, d), lambda l, i: (l, (i + 1) * (tr // N_HEADS), 0)),
        ],
        out_specs=pl.BlockSpec((None, tr, d), lambda l, i: (l, i, 0)),
        out_shape=jax.ShapeDtypeStruct((depth, N_MAIN, d), BF16),
        compiler_params=_params(("parallel", "parallel")),
        name="w_in_prep",
    )(w_in_t, w_in_t)


def _inproj(x2, g, w_main_t, w_f_t, layer, tm=1024, tn=2048):
    m, d = x2.shape
    n = w_main_t.shape[1]
    return pl.pallas_call(
        _inproj_kernel,
        grid=(m // tm, n // tn),
        in_specs=[
            pl.BlockSpec(memory_space=pl.ANY),
            pl.BlockSpec((1, d), lambda i, j: (0, 0)),
            pl.BlockSpec((None, tn, d), lambda i, j: (layer, j, 0)),
            pl.BlockSpec((None, LANES, d), lambda i, j: (layer, 0, 0)),
        ],
        out_specs=[
            pl.BlockSpec((tm, tn), lambda i, j: (i, j)),
            pl.BlockSpec((tm, LANES), lambda i, j: (i, 0)),
        ],
        out_shape=[
            jax.ShapeDtypeStruct((m, n), F32),
            jax.ShapeDtypeStruct((m, LANES), F32),
        ],
        scratch_shapes=[pltpu.VMEM((tm, d), F32), pltpu.VMEM((tm, d), BF16),
                        pltpu.SemaphoreType.DMA(())],
        compiler_params=_params(("arbitrary", "arbitrary")),
        name="inproj",
    )(x2, g, w_main_t, w_f_t)


def _rotate(t, cos, sin_signed):
    return t * cos + pltpu.roll(t, HEAD_DIM // 2, 1) * sin_signed


def _retention_body(q_ref, k_ref, v_ref, z_ref, cos_ref, sin_ref, decay_ref,
                    xi_ref, zeta_ref, gch_ref, gn_ref, o_ref,
                    q_scr, k_scr, kz_scr, v_scr, inner_scr, state_scr, out_scr):
    decay = decay_ref[0]
    xi = xi_ref[0]
    zeta = zeta_ref[0]
    g_chunk = gch_ref[0, 0:1, :]
    n_chunks = q_ref.shape[0] // CHUNK
    chunk_rows = [pl.ds(c * CHUNK, CHUNK) for c in range(n_chunks)]
    nt = (((1,), (1,)), ((), ()))
    tn = (((0,), (0,)), ((), ()))

    cos = cos_ref[...]
    sin = sin_ref[...]
    q_scr[...] = _rotate(q_ref[...], cos, sin).astype(BF16)
    k = _rotate(k_ref[...], cos, sin) * (HEAD_DIM ** -0.5)
    k_scr[...] = k.astype(BF16)
    v_scr[...] = v_ref[...].astype(BF16)
    for c, rows in enumerate(chunk_rows):
        kz_scr[rows, :] = (k[c * CHUNK:(c + 1) * CHUNK] * zeta).astype(BF16)

    increments = [lax.dot_general(kz_scr[rows, :], v_scr[rows, :], tn, preferred_element_type=F32)
                  for rows in chunk_rows]
    for rows in chunk_rows:
        inner = lax.dot_general(q_scr[rows, :], k_scr[rows, :], nt,
                                preferred_element_type=F32) * decay
        inner_scr[rows, :] = inner.astype(BF16)

    state = jnp.zeros((HEAD_DIM, HEAD_DIM), F32)
    for rows, inc in zip(chunk_rows, increments):
        state_scr[rows, :] = state.astype(BF16)
        state = g_chunk * state + inc

    for rows in chunk_rows:
        out_scr[rows, :] = (
            jnp.dot(inner_scr[rows, :], v_scr[rows, :], preferred_element_type=F32)
            + jnp.dot(q_scr[rows, :], state_scr[rows, :], preferred_element_type=F32) * xi)

    out = out_scr[...]
    mu = jnp.mean(out, axis=-1, keepdims=True)
    dev = out - mu
    var = jnp.mean(dev * dev, axis=-1, keepdims=True)
    yn = dev * lax.rsqrt(var + EPS) * gn_ref[0]
    z = z_ref[...]
    o_ref[...] = (yn * (z * jax.nn.sigmoid(z))).astype(o_ref.dtype)


def _retention_tables(seq):
    half = HEAD_DIM // 2
    inv = ROPE_THETA ** (-jnp.arange(half, dtype=F32) / half)
    ang = jnp.arange(seq, dtype=F32)[:, None] * inv[None, :]
    cos = jnp.cos(ang)
    sin = jnp.sin(ang)
    cos2 = jnp.concatenate([cos, cos], axis=-1)
    sin2 = jnp.concatenate([-sin, sin], axis=-1)
    log_g = jnp.log1p(-jnp.exp2(-5.0 - jnp.arange(N_HEADS, dtype=F32)))
    idx = jnp.arange(CHUNK, dtype=F32)
    diff = idx[:, None] - idx[None, :]
    decay = jnp.where(diff >= 0, jnp.exp(log_g[:, None, None] * jnp.maximum(diff, 0.0)), 0.0)
    xi = jnp.exp(log_g[:, None] * (idx + 1.0))
    zeta = jnp.exp(log_g[:, None] * (CHUNK - 1.0 - idx))
    g_chunk = jnp.exp(log_g * CHUNK)
    bcast = lambda t: jnp.broadcast_to(t[:, :, None], (N_HEADS, CHUNK, HEAD_DIM))
    gch = jnp.broadcast_to(g_chunk[:, None, None], (N_HEADS, 8, HEAD_DIM))
    return cos2, sin2, decay, bcast(xi), bcast(zeta), gch


def _split3(x):
    hi = x.astype(BF16)
    r1 = x - hi.astype(F32)
    mid = r1.astype(BF16)
    lo = (r1 - mid.astype(F32)).astype(BF16)
    return hi, mid, lo


def _forget_cumsum_kernel(f_ref, b_ref, col_ref, row_ref):
    n_blocks = f_ref.shape[0] // CHUNK
    r = lax.broadcasted_iota(jnp.int32, (CHUNK, CHUNK), 0)
    c = lax.broadcasted_iota(jnp.int32, (CHUNK, CHUNK), 1)
    tri = jnp.where(r >= c, 1.0, 0.0).astype(BF16)
    carry = jnp.zeros((1, LANES), F32)
    for blk in range(n_blocks):
        rows = pl.ds(blk * CHUNK, CHUNK)
        x = f_ref[rows, :] + b_ref[...]
        ls = jnp.minimum(x, 0.0) - jnp.log1p(jnp.exp(-jnp.abs(x)))
        hi, mid, lo = _split3(ls)
        cs = (jnp.dot(tri, hi, preferred_element_type=F32)
              + jnp.dot(tri, mid, preferred_element_type=F32)
              + jnp.dot(tri, lo, preferred_element_type=F32)) + carry
        col_ref[rows, :] = cs
        row_ref[0, :, rows] = cs.T
        carry = cs[CHUNK - 1:CHUNK, :]


def _forget_cumsum(f_logits, bias, batch, seq):
    m = f_logits.shape[0]
    return pl.pallas_call(
        _forget_cumsum_kernel,
        grid=(batch,),
        in_specs=[
            pl.BlockSpec((seq, LANES), lambda b: (b, 0)),
            pl.BlockSpec((1, LANES), lambda b: (0, 0)),
        ],
        out_specs=[
            pl.BlockSpec((seq, LANES), lambda b: (b, 0)),
            pl.BlockSpec((1, LANES, seq), lambda b: (b, 0, 0)),
        ],
        out_shape=[
            jax.ShapeDtypeStruct((m, LANES), F32),
            jax.ShapeDtypeStruct((batch, LANES, seq), F32),
        ],
        compiler_params=_params(("parallel",)),
        name="forget_cumsum",
    )(f_logits, bias)


def _lane_groups(x):
    return [x[:, g * LANES:(g + 1) * LANES] for g in range(x.shape[1] // LANES)]


def _fox_body(q_ref, k_ref, v_ref, z_ref, ccol_ref, crow_ref, o_ref,
              m_scr, acc_scr, *, tq):
    h = pl.program_id(1)
    seq = q_ref.shape[0]
    c1 = (HEAD_DIM ** -0.5) * LOG2E
    ones = jnp.ones((tq, HEAD_DIM), BF16)

    qb = q_ref[...].astype(BF16)
    lane = lax.broadcasted_iota(jnp.int32, (seq, LANES), 1)
    cq2 = jnp.sum(jnp.where(lane == h, ccol_ref[...], 0.0), axis=-1, keepdims=True) * LOG2E
    causal = (lax.broadcasted_iota(jnp.int32, (tq, tq), 0)
              >= lax.broadcasted_iota(jnp.int32, (tq, tq), 1))

    for j in range(seq // tq):
        lo = j * tq
        live = slice(lo, seq)
        kj = k_ref[lo:lo + tq, :].astype(BF16)
        vj = jnp.concatenate([v_ref[lo:lo + tq, :].astype(BF16), ones], axis=-1)
        s = lax.dot_general(qb[live], kj, (((1,), (1,)), ((), ())), preferred_element_type=F32)
        a = s * c1 - crow_ref[0, j] * LOG2E
        diag = jnp.where(causal, a[:tq], NEG_INF)
        a = diag if lo + tq == seq else jnp.concatenate([diag, a[tq:]], axis=0)
        groups = _lane_groups(a)
        m_blk = jnp.max(functools.reduce(jnp.maximum, groups), axis=-1, keepdims=True) + cq2[live]
        if j == 0:
            m_new = jnp.broadcast_to(m_blk, (seq, LANES))
        else:
            m_prev = m_scr[live, :]
            m_new = jnp.maximum(m_prev, m_blk)
            alpha = jnp.exp2(m_prev - m_new)
        shift = m_new - cq2[live]
        p = jnp.concatenate([jnp.exp2(g - shift).astype(BF16) for g in groups], axis=-1)
        pv = jnp.dot(p, vj, preferred_element_type=F32)
        m_scr[live, :] = m_new
        if j == 0:
            acc_scr[...] = pv
        else:
            acc_scr[live, :] = jnp.concatenate([alpha, alpha], axis=-1) * acc_scr[live, :] + pv

    z = z_ref[...]
    o_ref[...] = ((acc_scr[:, :HEAD_DIM] / acc_scr[:, HEAD_DIM:])
                  * (z * jax.nn.sigmoid(z))).astype(o_ref.dtype)


N_RET_IN, N_FOX_IN, N_RET_SCR = 11, 6, 7


def _mixers_kernel(*refs, tq):
    ret_in, refs = refs[:N_RET_IN], refs[N_RET_IN:]
    fox_in, refs = refs[:N_FOX_IN], refs[N_FOX_IN:]
    (y_ret_ref, y_fox_ref), refs = refs[:2], refs[2:]
    ret_scr, fox_scr = refs[:N_RET_SCR], refs[N_RET_SCR:]
    _fox_body(*fox_in, y_fox_ref, *fox_scr, tq=tq)
    _retention_body(*ret_in, y_ret_ref, *ret_scr)


def _mixers(proj, tables, gn_g, c_col, c_row, batch, seq, tq):
    cos, sin, decay, xi, zeta, gch = tables
    m = proj.shape[0]
    head_block = lambda off: pl.BlockSpec((seq, HEAD_DIM), lambda b, h: (b, off + h))
    per_head = lambda shape: pl.BlockSpec((1,) + shape, lambda b, h: (h, 0, 0))
    table = pl.BlockSpec((seq, HEAD_DIM), lambda b, h: (0, 0))
    y_spec = pl.BlockSpec((seq, HEAD_DIM), lambda b, h: (b, h))
    y_shape = jax.ShapeDtypeStruct((m, WIDTH), BF16)
    return pl.pallas_call(
        functools.partial(_mixers_kernel, tq=tq),
        grid=(batch, N_HEADS),
        in_specs=[
            head_block(COL_RET), head_block(COL_RET + N_HEADS),
            head_block(COL_RET + 2 * N_HEADS), head_block(COL_RET + 3 * N_HEADS),
            table, table,
            per_head((CHUNK, CHUNK)), per_head((CHUNK, HEAD_DIM)),
            per_head((CHUNK, HEAD_DIM)), per_head((8, HEAD_DIM)), per_head((1, HEAD_DIM)),
            head_block(COL_FOX), head_block(COL_FOX + N_HEADS),
            head_block(COL_FOX + 2 * N_HEADS), head_block(COL_FOX + 3 * N_HEADS),
            pl.BlockSpec((seq, LANES), lambda b, h: (b, 0)),
            pl.BlockSpec((1, seq // tq, 1, tq), lambda b, h: (b * N_HEADS + h, 0, 0, 0)),
        ],
        out_specs=[y_spec, y_spec],
        out_shape=[y_shape, y_shape],
        scratch_shapes=([pltpu.VMEM((seq, HEAD_DIM), BF16)] * 6 + [pltpu.VMEM((seq, HEAD_DIM), F32)]
                        + [pltpu.VMEM((seq, LANES), F32), pltpu.VMEM((seq, 2 * HEAD_DIM), F32)]),
        compiler_params=_params(("parallel", "arbitrary")),
        name="mixers",
    )(proj, proj, proj, proj, cos, sin, decay, xi, zeta, gch, gn_g.reshape(N_HEADS, 1, HEAD_DIM),
      proj, proj, proj, proj, c_col, c_row)


POOL_HALO = 16


def _pool_group(gi, u_ref, halo_ref, z_ref, w_ref, s_ref, tile_in_seq):
    tm = u_ref.shape[0]
    w = POOL_WINDOWS[gi]
    cols = slice(gi * POOL_GROUP_DIM, (gi + 1) * POOL_GROUP_DIM)
    halo = jnp.where(tile_in_seq == 0, 0.0, halo_ref[:, cols])
    t1 = (tile_in_seq * tm + lax.broadcasted_iota(jnp.int32, (tm, LANES), 0) + 1).astype(F32)
    u = u_ref[:, cols]
    total, k = jnp.concatenate([halo, u], axis=0), 1
    while k < w:
        total = total + pltpu.roll(total, k, 0)
        k *= 2
    inv = 1.0 / jnp.minimum(t1, float(w))
    mean = jnp.concatenate([x * inv for x in _lane_groups(total[POOL_HALO:])], axis=-1)
    mixed = jnp.dot((mean - u).astype(BF16), w_ref[gi], preferred_element_type=F32) * s_ref[gi]
    z = z_ref[:, cols]
    return (mixed * (z * jax.nn.sigmoid(z))).astype(BF16)


def _merge_kernel(x_ref, yr_ref, yf_ref, u_ref, halo_ref, pz_ref, ga_ref, gb_ref, gc_ref,
                  pw_ref, ps_ref, wr_ref, wf_ref, wp_ref, wo_ref, fg_ref, o_ref,
                  *, tiles_per_seq, final_norm):
    tile_in_seq = pl.program_id(0) % tiles_per_seq
    quarter = o_ref.shape[1] // 4
    pool = functools.partial(_pool_group, u_ref=u_ref, halo_ref=halo_ref, z_ref=pz_ref,
                             w_ref=pw_ref, s_ref=ps_ref, tile_in_seq=tile_in_seq)

    yr = yr_ref[...]
    br, yp = [], []
    for part in range(4):
        cols = slice(part * quarter, (part + 1) * quarter)
        br.append(jnp.dot(yr, wr_ref[:, cols], preferred_element_type=F32))
        yp.append(pool(part))
    merged = (jax.nn.sigmoid(ga_ref[...]) * jnp.concatenate(br, axis=-1)
              + jax.nn.sigmoid(gb_ref[...]) * jnp.dot(yf_ref[...], wf_ref[...], preferred_element_type=F32)
              + jax.nn.sigmoid(gc_ref[...]) * jnp.dot(jnp.concatenate(yp, axis=-1), wp_ref[...],
                                                      preferred_element_type=F32))
    x = x_ref[...] + jnp.dot(merged.astype(BF16), wo_ref[...], preferred_element_type=F32)
    if final_norm:
        x = x * lax.rsqrt(jnp.mean(x * x, axis=-1, keepdims=True) + EPS) * fg_ref[...]
    o_ref[...] = x


def _merge(x2, y_ret, y_fox, proj, pool_w, pool_scale, w_r, w_f, w_p, w_o, final_g,
           layer, seq, final_norm, tm=256):
    m, d = x2.shape
    n_groups = len(POOL_WINDOWS)
    halo_per_tile = tm // POOL_HALO
    row = lambda width: pl.BlockSpec((tm, width), lambda i: (i, 0))
    col = lambda width, off: pl.BlockSpec((tm, width), lambda i: (i, off * LANES // width))
    once = lambda shape, index: pl.BlockSpec(shape, index, pipeline_mode=pl.Buffered(1))
    weight = lambda rows: once((None, rows, d), lambda i: (layer, 0, 0))
    halo = pl.BlockSpec((POOL_HALO, WIDTH),
                        lambda i: (jnp.maximum(i * halo_per_tile - 1, 0), COL_POOL_U * LANES // WIDTH))
    return pl.pallas_call(
        functools.partial(_merge_kernel, tiles_per_seq=seq // tm, final_norm=final_norm),
        grid=(m // tm,),
        in_specs=[
            row(d), row(WIDTH), row(WIDTH),
            col(WIDTH, COL_POOL_U), halo, col(WIDTH, COL_POOL_Z),
            col(d, COL_GATES), col(d, COL_GATES + d // LANES), col(d, COL_GATES + 2 * d // LANES),
            once((None, n_groups, POOL_GROUP_DIM, POOL_GROUP_DIM), lambda i: (layer, 0, 0, 0)),
            once((None, n_groups, 1, POOL_GROUP_DIM), lambda i: (layer, 0, 0, 0)),
            weight(WIDTH), weight(WIDTH), weight(WIDTH), weight(d),
            once((1, d), lambda i: (0, 0)),
        ],
        out_specs=row(d),
        out_shape=jax.ShapeDtypeStruct((m, d), F32),
        compiler_params=_params(("parallel",)),
        name="merge_out",
    )(x2, y_ret, y_fox, proj, proj, proj, proj, proj, proj,
      pool_w, pool_scale, w_r, w_f, w_p, w_o, final_g)


def kernel(x, norm_g, w_in, ret_gn_g, fox_b_f, pool_w, pool_scale,
           w_ret_branch, w_fox_branch, w_pool_branch, w_out, final_g):
    batch, seq, d = x.shape
    depth = w_in.shape[0]
    tables = _retention_tables(seq)
    n_before = 8 * WIDTH
    w_in_t = jnp.swapaxes(w_in, 1, 2)
    w_main = _prep_w_in(w_in_t)
    w_forget = jnp.pad(w_in_t[:, n_before:n_before + N_HEADS, :],
                       ((0, 0), (0, LANES - N_HEADS), (0, 0))).astype(BF16)
    w_r, w_f, w_p, w_o = (t.astype(BF16) for t in (w_ret_branch, w_fox_branch, w_pool_branch, w_out))
    pool_w_b = pool_w.astype(BF16)
    pool_scale_r = pool_scale.reshape(depth, len(POOL_WINDOWS), 1, POOL_GROUP_DIM)
    x2 = x.reshape(batch * seq, d)
    for layer in range(depth):
        bias = jnp.pad(fox_b_f[layer], (0, LANES - N_HEADS)).reshape(1, LANES)

        proj, f_logits = _inproj(x2, norm_g[layer].reshape(1, d), w_main, w_forget, layer)
        c_col, c_row_t = _forget_cumsum(f_logits, bias, batch, seq)
        c_row = c_row_t[:, :N_HEADS, :].reshape(
            batch * N_HEADS, seq // FOX_BLOCK, 1, FOX_BLOCK)
        y_ret, y_fox = _mixers(proj, tables, ret_gn_g[layer], c_col, c_row, batch, seq, FOX_BLOCK)
        x2 = _merge(x2, y_ret, y_fox, proj, pool_w_b, pool_scale_r, w_r, w_f, w_p, w_o,
                    final_g.reshape(1, d), layer, seq, final_norm=(layer == depth - 1))
    return x2.reshape(batch, seq, d)
```

```python
import functools

import jax
import jax.numpy as jnp
from jax import lax
from jax.experimental import pallas as pl
from jax.experimental.pallas import tpu as pltpu

D_MODEL = 2048
HEAD_DIM = 128
N_HEADS = 8
WIDTH = D_MODEL // 2
POOL_WINDOWS = (2, 4, 8, 16)
POOL_GROUP_DIM = WIDTH // len(POOL_WINDOWS)
CHUNK = 128
ROPE_THETA = 10000.0
EPS = 1e-6
NEG_INF = -1e30
LOG2E = 1.4426950408889634
FOX_BLOCK = 512

LANES = 128
N_MAIN = 10 * WIDTH + 3 * D_MODEL
COL_RET = 0
COL_FOX = 4 * N_HEADS
COL_POOL_U = 8 * N_HEADS
COL_POOL_Z = 9 * N_HEADS
COL_GATES = 10 * N_HEADS

BF16 = jnp.bfloat16
F32 = jnp.float32
NT_DIMS = (((1,), (1,)), ((), ()))

VMEM_LIMIT = 56 * 1024 * 1024


def _params(sem, vmem=VMEM_LIMIT):
    return pltpu.CompilerParams(dimension_semantics=sem, vmem_limit_bytes=vmem)


def _inproj_kernel(x_hbm, g_ref, w_ref, wf_ref, *rest, prep_next):
    if prep_next:
        w_next_src, o_ref, f_ref, w_next_ref, x_buf, h_scr, x_sem = rest
    else:
        o_ref, f_ref, x_buf, h_scr, x_sem = rest
    i, j = pl.program_id(0), pl.program_id(1)
    tm = x_buf.shape[0]

    def x_copy(tile):
        return pltpu.make_async_copy(x_hbm.at[pl.ds(tile * tm, tm), :], x_buf, x_sem)

    @pl.when((i == 0) & (j == 0))
    def _():
        x_copy(0).start()

    @pl.when(j == 0)
    def _():
        x_copy(i).wait()
        x = x_buf[...]
        y = x * lax.rsqrt(jnp.mean(x * x, axis=-1, keepdims=True) + EPS)
        h = (y * g_ref[...]).astype(BF16)
        h_scr[...] = h
        f_ref[...] = lax.dot_general(h, wf_ref[...], NT_DIMS, preferred_element_type=F32)

    @pl.when((j == 1) & (i + 1 < pl.num_programs(0)))
    def _():
        x_copy(i + 1).start()

    h = h_scr[...]
    half = o_ref.shape[1] // 2
    o_ref[:, :half] = lax.dot_general(h, w_ref[:half, :], NT_DIMS, preferred_element_type=F32)
    if prep_next:
        w_next_ref[...] = w_next_src[0].astype(BF16)
    o_ref[:, half:] = lax.dot_general(h, w_ref[half:, :], NT_DIMS, preferred_element_type=F32)


def _w_rows_spec(layer, rows, step_of):
    def index(*ids):
        first = step_of(*ids) * rows
        return layer, pl.multiple_of(first + jnp.where(first >= 8 * WIDTH, N_HEADS, 0), N_HEADS), 0
    return pl.BlockSpec((pl.Element(1), pl.Element(rows), pl.Element(D_MODEL)), index)


def _wprep_kernel(src_ref, out_ref):
    out_ref[...] = src_ref[0].astype(BF16)


def _prep_w_in(w_in_t, layer, tr=512):
    d = w_in_t.shape[2]
    return pl.pallas_call(
        _wprep_kernel,
        grid=(N_MAIN // tr,),
        in_specs=[_w_rows_spec(layer, tr, lambda i: i)],
        out_specs=pl.BlockSpec((tr, d), lambda i: (i, 0)),
        out_shape=jax.ShapeDtypeStruct((N_MAIN, d), BF16),
        compiler_params=_params(("parallel",)),
        name="w_in_prep",
    )(w_in_t)


def _inproj(x2, g, w_main_t, w_f_t, w_in_t, layer, tm=1024, tn=2048):
    m, d = x2.shape
    n = w_main_t.shape[0]
    grid = (m // tm, n // tn)
    prep_next = layer + 1 < w_in_t.shape[0]
    in_specs = [
        pl.BlockSpec(memory_space=pl.ANY),
        pl.BlockSpec((1, d), lambda i, j: (0, 0)),
        pl.BlockSpec((tn, d), lambda i, j: (j, 0)),
        pl.BlockSpec((None, LANES, d), lambda i, j: (layer, 0, 0)),
    ]
    out_specs = [
        pl.BlockSpec((tm, tn), lambda i, j: (i, j)),
        pl.BlockSpec((tm, LANES), lambda i, j: (i, 0)),
    ]
    out_shape = [jax.ShapeDtypeStruct((m, n), F32), jax.ShapeDtypeStruct((m, LANES), F32)]
    operands = [x2, g, w_main_t, w_f_t]
    if prep_next:
        rows = N_MAIN // (grid[0] * grid[1])
        step = lambda i, j: i * grid[1] + j
        in_specs.append(_w_rows_spec(layer + 1, rows, step))
        out_specs.append(pl.BlockSpec((rows, d), lambda i, j: (step(i, j), 0)))
        out_shape.append(jax.ShapeDtypeStruct((N_MAIN, d), BF16))
        operands.append(w_in_t)
    outs = pl.pallas_call(
        functools.partial(_inproj_kernel, prep_next=prep_next),
        grid=grid,
        in_specs=in_specs,
        out_specs=out_specs,
        out_shape=out_shape,
        scratch_shapes=[pltpu.VMEM((tm, d), F32), pltpu.VMEM((tm, d), BF16),
                        pltpu.SemaphoreType.DMA(())],
        compiler_params=_params(("arbitrary", "arbitrary")),
        name="inproj",
    )(*operands)
    return tuple(outs) if prep_next else (*outs, None)


def _rotate(t, cos, sin_signed):
    return t * cos + pltpu.roll(t, HEAD_DIM // 2, 1) * sin_signed


def _retention_body(q_ref, k_ref, v_ref, z_ref, cos_ref, sin_ref, decay_ref,
                    xi_ref, zeta_ref, gch_ref, gn_ref, o_ref,
                    q_scr, k_scr, kz_scr, v_scr, inner_scr, state_scr, out_scr):
    decay = decay_ref[0]
    xi = xi_ref[0]
    zeta = zeta_ref[0]
    g_chunk = gch_ref[0, 0:1, :]
    n_chunks = q_ref.shape[0] // CHUNK
    chunk_rows = [pl.ds(c * CHUNK, CHUNK) for c in range(n_chunks)]
    nt = (((1,), (1,)), ((), ()))
    tn = (((0,), (0,)), ((), ()))

    cos = cos_ref[...]
    sin = sin_ref[...]
    q_scr[...] = _rotate(q_ref[...], cos, sin).astype(BF16)
    k = _rotate(k_ref[...], cos, sin) * (HEAD_DIM ** -0.5)
    k_scr[...] = k.astype(BF16)
    v_scr[...] = v_ref[...].astype(BF16)
    for c, rows in enumerate(chunk_rows):
        kz_scr[rows, :] = (k[c * CHUNK:(c + 1) * CHUNK] * zeta).astype(BF16)

    increments = [lax.dot_general(kz_scr[rows, :], v_scr[rows, :], tn, preferred_element_type=F32)
                  for rows in chunk_rows]
    for rows in chunk_rows:
        inner = lax.dot_general(q_scr[rows, :], k_scr[rows, :], nt,
                                preferred_element_type=F32) * decay
        inner_scr[rows, :] = inner.astype(BF16)

    state = jnp.zeros((HEAD_DIM, HEAD_DIM), F32)
    for rows, inc in zip(chunk_rows, increments):
        state_scr[rows, :] = state.astype(BF16)
        state = g_chunk * state + inc

    for rows in chunk_rows:
        out_scr[rows, :] = (
            jnp.dot(inner_scr[rows, :], v_scr[rows, :], preferred_element_type=F32)
            + jnp.dot(q_scr[rows, :], state_scr[rows, :], preferred_element_type=F32) * xi)

    out = out_scr[...]
    mu = jnp.mean(out, axis=-1, keepdims=True)
    dev = out - mu
    var = jnp.mean(dev * dev, axis=-1, keepdims=True)
    yn = dev * lax.rsqrt(var + EPS) * gn_ref[0]
    z = z_ref[...]
    o_ref[...] = (yn * (z * jax.nn.sigmoid(z))).astype(o_ref.dtype)


def _retention_tables(seq):
    half = HEAD_DIM // 2
    inv = ROPE_THETA ** (-jnp.arange(half, dtype=F32) / half)
    ang = jnp.arange(seq, dtype=F32)[:, None] * inv[None, :]
    cos = jnp.cos(ang)
    sin = jnp.sin(ang)
    cos2 = jnp.concatenate([cos, cos], axis=-1)
    sin2 = jnp.concatenate([-sin, sin], axis=-1)
    log_g = jnp.log1p(-jnp.exp2(-5.0 - jnp.arange(N_HEADS, dtype=F32)))
    idx = jnp.arange(CHUNK, dtype=F32)
    diff = idx[:, None] - idx[None, :]
    decay = jnp.where(diff >= 0, jnp.exp(log_g[:, None, None] * jnp.maximum(diff, 0.0)), 0.0)
    xi = jnp.exp(log_g[:, None] * (idx + 1.0))
    zeta = jnp.exp(log_g[:, None] * (CHUNK - 1.0 - idx))
    g_chunk = jnp.exp(log_g * CHUNK)
    bcast = lambda t: jnp.broadcast_to(t[:, :, None], (N_HEADS, CHUNK, HEAD_DIM))
    gch = jnp.broadcast_to(g_chunk[:, None, None], (N_HEADS, 8, HEAD_DIM))
    return cos2, sin2, decay, bcast(xi), bcast(zeta), gch


def _split3(x):
    hi = x.astype(BF16)
    r1 = x - hi.astype(F32)
    mid = r1.astype(BF16)
    lo = (r1 - mid.astype(F32)).astype(BF16)
    return hi, mid, lo


def _forget_cumsum_kernel(f_ref, b_ref, col_ref, row_ref):
    n_blocks = f_ref.shape[0] // CHUNK
    r = lax.broadcasted_iota(jnp.int32, (CHUNK, CHUNK), 0)
    c = lax.broadcasted_iota(jnp.int32, (CHUNK, CHUNK), 1)
    tri = jnp.where(r >= c, 1.0, 0.0).astype(BF16)
    carry = jnp.zeros((1, LANES), F32)
    for blk in range(n_blocks):
        rows = pl.ds(blk * CHUNK, CHUNK)
        x = f_ref[rows, :] + b_ref[...]
        ls = jnp.minimum(x, 0.0) - jnp.log1p(jnp.exp(-jnp.abs(x)))
        hi, mid, lo = _split3(ls)
        cs = (jnp.dot(tri, hi, preferred_element_type=F32)
              + jnp.dot(tri, mid, preferred_element_type=F32)
              + jnp.dot(tri, lo, preferred_element_type=F32)) + carry
        col_ref[rows, :] = cs
        row_ref[0, :, rows] = cs.T
        carry = cs[CHUNK - 1:CHUNK, :]


def _forget_cumsum(f_logits, bias, batch, seq):
    m = f_logits.shape[0]
    return pl.pallas_call(
        _forget_cumsum_kernel,
        grid=(batch,),
        in_specs=[
            pl.BlockSpec((seq, LANES), lambda b: (b, 0)),
            pl.BlockSpec((1, LANES), lambda b: (0, 0)),
        ],
        out_specs=[
            pl.BlockSpec((seq, LANES), lambda b: (b, 0)),
            pl.BlockSpec((1, LANES, seq), lambda b: (b, 0, 0)),
        ],
        out_shape=[
            jax.ShapeDtypeStruct((m, LANES), F32),
            jax.ShapeDtypeStruct((batch, LANES, seq), F32),
        ],
        compiler_params=_params(("parallel",)),
        name="forget_cumsum",
    )(f_logits, bias)


def _lane_groups(x):
    return [x[:, g * LANES:(g + 1) * LANES] for g in range(x.shape[1] // LANES)]


def _fox_body(q_ref, k_ref, v_ref, z_ref, ccol_ref, crow_ref, o_ref,
              m_scr, acc_scr, *, tq):
    h = pl.program_id(1)
    seq = q_ref.shape[0]
    c1 = (HEAD_DIM ** -0.5) * LOG2E
    ones = jnp.ones((tq, HEAD_DIM), BF16)

    qb = q_ref[...].astype(BF16)
    lane = lax.broadcasted_iota(jnp.int32, (seq, LANES), 1)
    cq2 = jnp.sum(jnp.where(lane == h, ccol_ref[...], 0.0), axis=-1, keepdims=True) * LOG2E
    causal = (lax.broadcasted_iota(jnp.int32, (tq, tq), 0)
              >= lax.broadcasted_iota(jnp.int32, (tq, tq), 1))

    for j in range(seq // tq):
        lo = j * tq
        live = slice(lo, seq)
        kj = k_ref[lo:lo + tq, :].astype(BF16)
        vj = jnp.concatenate([v_ref[lo:lo + tq, :].astype(BF16), ones], axis=-1)
        s = lax.dot_general(qb[live], kj, (((1,), (1,)), ((), ())), preferred_element_type=F32)
        a = s * c1 - crow_ref[0, j] * LOG2E
        diag = jnp.where(causal, a[:tq], NEG_INF)
        a = diag if lo + tq == seq else jnp.concatenate([diag, a[tq:]], axis=0)
        groups = _lane_groups(a)
        m_blk = jnp.max(functools.reduce(jnp.maximum, groups), axis=-1, keepdims=True) + cq2[live]
        if j == 0:
            m_new = jnp.broadcast_to(m_blk, (seq, LANES))
        else:
            m_prev = m_scr[live, :]
            m_new = jnp.maximum(m_prev, m_blk)
            alpha = jnp.exp2(m_prev - m_new)
        shift = m_new - cq2[live]
        p = jnp.concatenate([jnp.exp2(g - shift).astype(BF16) for g in groups], axis=-1)
        pv = jnp.dot(p, vj, preferred_element_type=F32)
        m_scr[live, :] = m_new
        if j == 0:
            acc_scr[...] = pv
        else:
            acc_scr[live, :] = jnp.concatenate([alpha, alpha], axis=-1) * acc_scr[live, :] + pv

    z = z_ref[...]
    o_ref[...] = ((acc_scr[:, :HEAD_DIM] / acc_scr[:, HEAD_DIM:])
                  * (z * jax.nn.sigmoid(z))).astype(o_ref.dtype)


N_RET_IN, N_FOX_IN, N_RET_SCR = 11, 6, 7


def _mixers_kernel(*refs, tq):
    ret_in, refs = refs[:N_RET_IN], refs[N_RET_IN:]
    fox_in, refs = refs[:N_FOX_IN], refs[N_FOX_IN:]
    (y_ret_ref, y_fox_ref), refs = refs[:2], refs[2:]
    ret_scr, fox_scr = refs[:N_RET_SCR], refs[N_RET_SCR:]
    _fox_body(*fox_in, y_fox_ref, *fox_scr, tq=tq)
    _retention_body(*ret_in, y_ret_ref, *ret_scr)


def _mixers(proj, tables, gn_g, c_col, c_row, batch, seq, tq):
    cos, sin, decay, xi, zeta, gch = tables
    m = proj.shape[0]
    head_block = lambda off: pl.BlockSpec((seq, HEAD_DIM), lambda b, h: (b, off + h))
    per_head = lambda shape: pl.BlockSpec((1,) + shape, lambda b, h: (h, 0, 0))
    table = pl.BlockSpec((seq, HEAD_DIM), lambda b, h: (0, 0))
    y_spec = pl.BlockSpec((seq, HEAD_DIM), lambda b, h: (b, h))
    y_shape = jax.ShapeDtypeStruct((m, WIDTH), BF16)
    return pl.pallas_call(
        functools.partial(_mixers_kernel, tq=tq),
        grid=(batch, N_HEADS),
        in_specs=[
            head_block(COL_RET), head_block(COL_RET + N_HEADS),
            head_block(COL_RET + 2 * N_HEADS), head_block(COL_RET + 3 * N_HEADS),
            table, table,
            per_head((CHUNK, CHUNK)), per_head((CHUNK, HEAD_DIM)),
            per_head((CHUNK, HEAD_DIM)), per_head((8, HEAD_DIM)), per_head((1, HEAD_DIM)),
            head_block(COL_FOX), head_block(COL_FOX + N_HEADS),
            head_block(COL_FOX + 2 * N_HEADS), head_block(COL_FOX + 3 * N_HEADS),
            pl.BlockSpec((seq, LANES), lambda b, h: (b, 0)),
            pl.BlockSpec((1, seq // tq, 1, tq), lambda b, h: (b * N_HEADS + h, 0, 0, 0)),
        ],
        out_specs=[y_spec, y_spec],
        out_shape=[y_shape, y_shape],
        scratch_shapes=([pltpu.VMEM((seq, HEAD_DIM), BF16)] * 6 + [pltpu.VMEM((seq, HEAD_DIM), F32)]
                        + [pltpu.VMEM((seq, LANES), F32), pltpu.VMEM((seq, 2 * HEAD_DIM), F32)]),
        compiler_params=_params(("parallel", "arbitrary")),
        name="mixers",
    )(proj, proj, proj, proj, cos, sin, decay, xi, zeta, gch, gn_g.reshape(N_HEADS, 1, HEAD_DIM),
      proj, proj, proj, proj, c_col, c_row)


POOL_HALO = 16


def _pool_group(gi, u_ref, halo_ref, z_ref, w_ref, s_ref, tile_in_seq):
    tm = u_ref.shape[0]
    w = POOL_WINDOWS[gi]
    cols = slice(gi * POOL_GROUP_DIM, (gi + 1) * POOL_GROUP_DIM)
    halo = jnp.where(tile_in_seq == 0, 0.0, halo_ref[:, cols])
    t1 = (tile_in_seq * tm + lax.broadcasted_iota(jnp.int32, (tm, LANES), 0) + 1).astype(F32)
    u = u_ref[:, cols]
    total, k = jnp.concatenate([halo, u], axis=0), 1
    while k < w:
        total = total + pltpu.roll(total, k, 0)
        k *= 2
    inv = 1.0 / jnp.minimum(t1, float(w))
    mean = jnp.concatenate([x * inv for x in _lane_groups(total[POOL_HALO:])], axis=-1)
    mixed = jnp.dot((mean - u).astype(BF16), w_ref[gi], preferred_element_type=F32) * s_ref[gi]
    z = z_ref[:, cols]
    return (mixed * (z * jax.nn.sigmoid(z))).astype(BF16)


def _merge_kernel(x_ref, yr_ref, yf_ref, u_ref, halo_ref, pz_ref, ga_ref, gb_ref, gc_ref,
                  pw_ref, ps_ref, wr_ref, wf_ref, wp_ref, wo_ref, fg_ref, o_ref,
                  *, tiles_per_seq, final_norm):
    tile_in_seq = pl.program_id(0) % tiles_per_seq
    quarter = o_ref.shape[1] // 4
    pool = functools.partial(_pool_group, u_ref=u_ref, halo_ref=halo_ref, z_ref=pz_ref,
                             w_ref=pw_ref, s_ref=ps_ref, tile_in_seq=tile_in_seq)

    yr = yr_ref[...]
    br, yp = [], []
    for part in range(4):
        cols = slice(part * quarter, (part + 1) * quarter)
        br.append(jnp.dot(yr, wr_ref[:, cols], preferred_element_type=F32))
        yp.append(pool(part))
    merged = (jax.nn.sigmoid(ga_ref[...]) * jnp.concatenate(br, axis=-1)
              + jax.nn.sigmoid(gb_ref[...]) * jnp.dot(yf_ref[...], wf_ref[...], preferred_element_type=F32)
              + jax.nn.sigmoid(gc_ref[...]) * jnp.dot(jnp.concatenate(yp, axis=-1), wp_ref[...],
                                                      preferred_element_type=F32))
    x = x_ref[...] + jnp.dot(merged.astype(BF16), wo_ref[...], preferred_element_type=F32)
    if final_norm:
        x = x * lax.rsqrt(jnp.mean(x * x, axis=-1, keepdims=True) + EPS) * fg_ref[...]
    o_ref[...] = x


def _merge(x2, y_ret, y_fox, proj, pool_w, pool_scale, w_r, w_f, w_p, w_o, final_g,
           layer, seq, final_norm, tm=256):
    m, d = x2.shape
    n_groups = len(POOL_WINDOWS)
    halo_per_tile = tm // POOL_HALO
    row = lambda width: pl.BlockSpec((tm, width), lambda i: (i, 0))
    col = lambda width, off: pl.BlockSpec((tm, width), lambda i: (i, off * LANES // width))
    once = lambda shape, index: pl.BlockSpec(shape, index, pipeline_mode=pl.Buffered(1))
    weight = lambda rows: once((None, rows, d), lambda i: (layer, 0, 0))
    halo = pl.BlockSpec((POOL_HALO, WIDTH),
                        lambda i: (jnp.maximum(i * halo_per_tile - 1, 0), COL_POOL_U * LANES // WIDTH))
    return pl.pallas_call(
        functools.partial(_merge_kernel, tiles_per_seq=seq // tm, final_norm=final_norm),
        grid=(m // tm,),
        in_specs=[
            row(d), row(WIDTH), row(WIDTH),
            col(WIDTH, COL_POOL_U), halo, col(WIDTH, COL_POOL_Z),
            col(d, COL_GATES), col(d, COL_GATES + d // LANES), col(d, COL_GATES + 2 * d // LANES),
            once((None, n_groups, POOL_GROUP_DIM, POOL_GROUP_DIM), lambda i: (layer, 0, 0, 0)),
            once((None, n_groups, 1, POOL_GROUP_DIM), lambda i: (layer, 0, 0, 0)),
            weight(WIDTH), weight(WIDTH), weight(WIDTH), weight(d),
            once((1, d), lambda i: (0, 0)),
        ],
        out_specs=row(d),
        out_shape=jax.ShapeDtypeStruct((m, d), F32),
        compiler_params=_params(("parallel",)),
        name="merge_out",
    )(x2, y_ret, y_fox, proj, proj, proj, proj, proj, proj,
      pool_w, pool_scale, w_r, w_f, w_p, w_o, final_g)


def kernel(x, norm_g, w_in, ret_gn_g, fox_b_f, pool_w, pool_scale,
           w_ret_branch, w_fox_branch, w_pool_branch, w_out, final_g):
    batch, seq, d = x.shape
    depth = w_in.shape[0]
    tables = _retention_tables(seq)
    n_before = 8 * WIDTH
    w_in_t = jnp.swapaxes(w_in, 1, 2)
    w_main = _prep_w_in(w_in_t, 0)
    w_forget = jnp.pad(w_in_t[:, n_before:n_before + N_HEADS, :],
                       ((0, 0), (0, LANES - N_HEADS), (0, 0))).astype(BF16)
    w_r, w_f, w_p, w_o = (t.astype(BF16) for t in (w_ret_branch, w_fox_branch, w_pool_branch, w_out))
    pool_w_b = pool_w.astype(BF16)
    pool_scale_r = pool_scale.reshape(depth, len(POOL_WINDOWS), 1, POOL_GROUP_DIM)
    x2 = x.reshape(batch * seq, d)
    for layer in range(depth):
        bias = jnp.pad(fox_b_f[layer], (0, LANES - N_HEADS)).reshape(1, LANES)

        proj, f_logits, w_main = _inproj(x2, norm_g[layer].reshape(1, d), w_main, w_forget,
                                         w_in_t, layer)
        c_col, c_row_t = _forget_cumsum(f_logits, bias, batch, seq)
        c_row = c_row_t[:, :N_HEADS, :].reshape(
            batch * N_HEADS, seq // FOX_BLOCK, 1, FOX_BLOCK)
        y_ret, y_fox = _mixers(proj, tables, ret_gn_g[layer], c_col, c_row, batch, seq, FOX_BLOCK)
        x2 = _merge(x2, y_ret, y_fox, proj, pool_w_b, pool_scale_r, w_r, w_f, w_p, w_o,
                    final_g.reshape(1, d), layer, seq, final_norm=(layer == depth - 1))
    return x2.reshape(batch, seq, d)
```

```python
import functools

import jax
import jax.numpy as jnp
from jax import lax
from jax.experimental import pallas as pl
from jax.experimental.pallas import tpu as pltpu

D_MODEL = 2048
HEAD_DIM = 128
N_HEADS = 8
WIDTH = D_MODEL // 2
POOL_WINDOWS = (2, 4, 8, 16)
POOL_GROUP_DIM = WIDTH // len(POOL_WINDOWS)
CHUNK = 128
ROPE_THETA = 10000.0
EPS = 1e-6
NEG_INF = -1e30
LOG2E = 1.4426950408889634
FOX_BLOCK = 512

LANES = 128
N_MAIN = 10 * WIDTH + 3 * D_MODEL
COL_RET = 0
COL_FOX = 4 * N_HEADS
COL_POOL_U = 8 * N_HEADS
COL_POOL_Z = 9 * N_HEADS
COL_GATES = 10 * N_HEADS

BF16 = jnp.bfloat16
F32 = jnp.float32
NT_DIMS = (((1,), (1,)), ((), ()))

VMEM_LIMIT = 56 * 1024 * 1024


def _params(sem, vmem=VMEM_LIMIT):
    return pltpu.CompilerParams(dimension_semantics=sem, vmem_limit_bytes=vmem)


def _inproj_kernel(x_hbm, g_ref, w_ref, wf_ref, *rest, n_casts):
    cast_srcs, rest = rest[:n_casts], rest[n_casts:]
    (o_ref, f_ref), rest = rest[:2], rest[2:]
    cast_dsts, (x_buf, h_scr, x_sem) = rest[:n_casts], rest[n_casts:]
    i, j = pl.program_id(0), pl.program_id(1)
    tm = x_buf.shape[0]

    def x_copy(tile):
        return pltpu.make_async_copy(x_hbm.at[pl.ds(tile * tm, tm), :], x_buf, x_sem)

    @pl.when((i == 0) & (j == 0))
    def _():
        x_copy(0).start()

    @pl.when(j == 0)
    def _():
        x_copy(i).wait()
        x = x_buf[...]
        y = x * lax.rsqrt(jnp.mean(x * x, axis=-1, keepdims=True) + EPS)
        h = (y * g_ref[...]).astype(BF16)
        h_scr[...] = h
        f_ref[...] = lax.dot_general(h, wf_ref[...], NT_DIMS, preferred_element_type=F32)

    @pl.when((j == 1) & (i + 1 < pl.num_programs(0)))
    def _():
        x_copy(i + 1).start()

    h = h_scr[...]
    half = o_ref.shape[1] // 2
    o_ref[:, :half] = lax.dot_general(h, w_ref[:half, :], NT_DIMS, preferred_element_type=F32)
    for src, dst in zip(cast_srcs, cast_dsts):
        dst[...] = src[...].reshape(dst.shape).astype(BF16)
    o_ref[:, half:] = lax.dot_general(h, w_ref[half:, :], NT_DIMS, preferred_element_type=F32)


def _w_rows_spec(layer, rows, step_of):
    def index(*ids):
        first = step_of(*ids) * rows
        return layer, pl.multiple_of(first + jnp.where(first >= 8 * WIDTH, N_HEADS, 0), N_HEADS), 0
    return pl.BlockSpec((pl.Element(1), pl.Element(rows), pl.Element(D_MODEL)), index)


def _wprep_kernel(src_ref, out_ref):
    out_ref[...] = src_ref[0].astype(BF16)


def _prep_w_in(w_in_t, layer, tr=512):
    d = w_in_t.shape[2]
    return pl.pallas_call(
        _wprep_kernel,
        grid=(N_MAIN // tr,),
        in_specs=[_w_rows_spec(layer, tr, lambda i: i)],
        out_specs=pl.BlockSpec((tr, d), lambda i: (i, 0)),
        out_shape=jax.ShapeDtypeStruct((N_MAIN, d), BF16),
        compiler_params=_params(("parallel",)),
        name="w_in_prep",
    )(w_in_t)


def _inproj(x2, g, w_main_t, w_f_t, w_in_t, layer, to_bf16=(), tm=1024, tn=2048):
    m, d = x2.shape
    n = w_main_t.shape[0]
    grid = (m // tm, n // tn)
    n_steps = grid[0] * grid[1]
    step = lambda i, j: i * grid[1] + j
    prep_next = layer + 1 < w_in_t.shape[0]
    in_specs = [
        pl.BlockSpec(memory_space=pl.ANY),
        pl.BlockSpec((1, d), lambda i, j: (0, 0)),
        pl.BlockSpec((tn, d), lambda i, j: (j, 0)),
        pl.BlockSpec((None, LANES, d), lambda i, j: (layer, 0, 0)),
    ]
    out_specs = [
        pl.BlockSpec((tm, tn), lambda i, j: (i, j)),
        pl.BlockSpec((tm, LANES), lambda i, j: (i, 0)),
    ]
    out_shape = [jax.ShapeDtypeStruct((m, n), F32), jax.ShapeDtypeStruct((m, LANES), F32)]
    operands = [x2, g, w_main_t, w_f_t]
    row_block = lambda rows: pl.BlockSpec((rows, d), lambda i, j: (step(i, j), 0))
    if prep_next:
        in_specs.append(_w_rows_spec(layer + 1, N_MAIN // n_steps, step))
        out_specs.append(row_block(N_MAIN // n_steps))
        out_shape.append(jax.ShapeDtypeStruct((N_MAIN, d), BF16))
        operands.append(w_in_t)
    for t in to_bf16:
        in_specs.append(row_block(t.shape[0] // n_steps))
        out_specs.append(row_block(t.shape[0] // n_steps))
        out_shape.append(jax.ShapeDtypeStruct(t.shape, BF16))
        operands.append(t)
    outs = pl.pallas_call(
        functools.partial(_inproj_kernel, n_casts=len(operands) - 4),
        grid=grid,
        in_specs=in_specs,
        out_specs=out_specs,
        out_shape=out_shape,
        scratch_shapes=[pltpu.VMEM((tm, d), F32), pltpu.VMEM((tm, d), BF16),
                        pltpu.SemaphoreType.DMA(())],
        compiler_params=_params(("arbitrary", "arbitrary")),
        name="inproj",
    )(*operands)
    proj, f_logits, *casts = outs
    w_next = casts.pop(0) if prep_next else None
    return proj, f_logits, w_next, casts


def _rotate(t, cos, sin_signed):
    return t * cos + pltpu.roll(t, HEAD_DIM // 2, 1) * sin_signed


def _retention_body(q_ref, k_ref, v_ref, z_ref, cos_ref, sin_ref, decay_ref,
                    xi_ref, zeta_ref, gch_ref, gn_ref, o_ref,
                    q_scr, k_scr, kz_scr, v_scr, inner_scr, state_scr, out_scr):
    decay = decay_ref[0]
    xi = xi_ref[0]
    zeta = zeta_ref[0]
    g_chunk = gch_ref[0, 0:1, :]
    n_chunks = q_ref.shape[0] // CHUNK
    chunk_rows = [pl.ds(c * CHUNK, CHUNK) for c in range(n_chunks)]
    nt = (((1,), (1,)), ((), ()))
    tn = (((0,), (0,)), ((), ()))

    cos = cos_ref[...]
    sin = sin_ref[...]
    q_scr[...] = _rotate(q_ref[...], cos, sin).astype(BF16)
    k = _rotate(k_ref[...], cos, sin) * (HEAD_DIM ** -0.5)
    k_scr[...] = k.astype(BF16)
    v_scr[...] = v_ref[...].astype(BF16)
    for c, rows in enumerate(chunk_rows):
        kz_scr[rows, :] = (k[c * CHUNK:(c + 1) * CHUNK] * zeta).astype(BF16)

    increments = [lax.dot_general(kz_scr[rows, :], v_scr[rows, :], tn, preferred_element_type=F32)
                  for rows in chunk_rows]
    for rows in chunk_rows:
        inner = lax.dot_general(q_scr[rows, :], k_scr[rows, :], nt,
                                preferred_element_type=F32) * decay
        inner_scr[rows, :] = inner.astype(BF16)

    state = jnp.zeros((HEAD_DIM, HEAD_DIM), F32)
    for rows, inc in zip(chunk_rows, increments):
        state_scr[rows, :] = state.astype(BF16)
        state = g_chunk * state + inc

    for rows in chunk_rows:
        out_scr[rows, :] = (
            jnp.dot(inner_scr[rows, :], v_scr[rows, :], preferred_element_type=F32)
            + jnp.dot(q_scr[rows, :], state_scr[rows, :], preferred_element_type=F32) * xi)

    out = out_scr[...]
    mu = jnp.mean(out, axis=-1, keepdims=True)
    dev = out - mu
    var = jnp.mean(dev * dev, axis=-1, keepdims=True)
    yn = dev * lax.rsqrt(var + EPS) * gn_ref[0]
    z = z_ref[...]
    o_ref[...] = (yn * (z * jax.nn.sigmoid(z))).astype(o_ref.dtype)


def _retention_tables(seq):
    half = HEAD_DIM // 2
    inv = ROPE_THETA ** (-jnp.arange(half, dtype=F32) / half)
    ang = jnp.arange(seq, dtype=F32)[:, None] * inv[None, :]
    cos = jnp.cos(ang)
    sin = jnp.sin(ang)
    cos2 = jnp.concatenate([cos, cos], axis=-1)
    sin2 = jnp.concatenate([-sin, sin], axis=-1)
    log_g = jnp.log1p(-jnp.exp2(-5.0 - jnp.arange(N_HEADS, dtype=F32)))
    idx = jnp.arange(CHUNK, dtype=F32)
    diff = idx[:, None] - idx[None, :]
    decay = jnp.where(diff >= 0, jnp.exp(log_g[:, None, None] * jnp.maximum(diff, 0.0)), 0.0)
    xi = jnp.exp(log_g[:, None] * (idx + 1.0))
    zeta = jnp.exp(log_g[:, None] * (CHUNK - 1.0 - idx))
    g_chunk = jnp.exp(log_g * CHUNK)
    bcast = lambda t: jnp.broadcast_to(t[:, :, None], (N_HEADS, CHUNK, HEAD_DIM))
    gch = jnp.broadcast_to(g_chunk[:, None, None], (N_HEADS, 8, HEAD_DIM))
    return cos2, sin2, decay, bcast(xi), bcast(zeta), gch


def _split3(x):
    hi = x.astype(BF16)
    r1 = x - hi.astype(F32)
    mid = r1.astype(BF16)
    lo = (r1 - mid.astype(F32)).astype(BF16)
    return hi, mid, lo


def _forget_cumsum_kernel(f_ref, b_ref, col_ref, row_ref):
    n_blocks = f_ref.shape[0] // CHUNK
    r = lax.broadcasted_iota(jnp.int32, (CHUNK, CHUNK), 0)
    c = lax.broadcasted_iota(jnp.int32, (CHUNK, CHUNK), 1)
    tri = jnp.where(r >= c, 1.0, 0.0).astype(BF16)
    carry = jnp.zeros((1, LANES), F32)
    for blk in range(n_blocks):
        rows = pl.ds(blk * CHUNK, CHUNK)
        x = f_ref[rows, :] + b_ref[...]
        ls = jnp.minimum(x, 0.0) - jnp.log1p(jnp.exp(-jnp.abs(x)))
        hi, mid, lo = _split3(ls)
        cs = (jnp.dot(tri, hi, preferred_element_type=F32)
              + jnp.dot(tri, mid, preferred_element_type=F32)
              + jnp.dot(tri, lo, preferred_element_type=F32)) + carry
        col_ref[rows, :] = cs
        row_ref[0, :, rows] = cs.T
        carry = cs[CHUNK - 1:CHUNK, :]


def _forget_cumsum(f_logits, bias, batch, seq):
    m = f_logits.shape[0]
    return pl.pallas_call(
        _forget_cumsum_kernel,
        grid=(batch,),
        in_specs=[
            pl.BlockSpec((seq, LANES), lambda b: (b, 0)),
            pl.BlockSpec((1, LANES), lambda b: (0, 0)),
        ],
        out_specs=[
            pl.BlockSpec((seq, LANES), lambda b: (b, 0)),
            pl.BlockSpec((1, LANES, seq), lambda b: (b, 0, 0)),
        ],
        out_shape=[
            jax.ShapeDtypeStruct((m, LANES), F32),
            jax.ShapeDtypeStruct((batch, LANES, seq), F32),
        ],
        compiler_params=_params(("parallel",)),
        name="forget_cumsum",
    )(f_logits, bias)


def _lane_groups(x):
    return [x[:, g * LANES:(g + 1) * LANES] for g in range(x.shape[1] // LANES)]


def _fox_body(q_ref, k_ref, v_ref, z_ref, ccol_ref, crow_ref, o_ref,
              m_scr, acc_scr, *, tq):
    h = pl.program_id(1)
    seq = q_ref.shape[0]
    c1 = (HEAD_DIM ** -0.5) * LOG2E
    ones = jnp.ones((tq, HEAD_DIM), BF16)

    qb = q_ref[...].astype(BF16)
    lane = lax.broadcasted_iota(jnp.int32, (seq, LANES), 1)
    cq2 = jnp.sum(jnp.where(lane == h, ccol_ref[...], 0.0), axis=-1, keepdims=True) * LOG2E
    causal = (lax.broadcasted_iota(jnp.int32, (tq, tq), 0)
              >= lax.broadcasted_iota(jnp.int32, (tq, tq), 1))

    for j in range(seq // tq):
        lo = j * tq
        live = slice(lo, seq)
        kj = k_ref[lo:lo + tq, :].astype(BF16)
        vj = jnp.concatenate([v_ref[lo:lo + tq, :].astype(BF16), ones], axis=-1)
        s = lax.dot_general(qb[live], kj, (((1,), (1,)), ((), ())), preferred_element_type=F32)
        a = s * c1 - crow_ref[0, j] * LOG2E
        diag = jnp.where(causal, a[:tq], NEG_INF)
        a = diag if lo + tq == seq else jnp.concatenate([diag, a[tq:]], axis=0)
        groups = _lane_groups(a)
        m_blk = jnp.max(functools.reduce(jnp.maximum, groups), axis=-1, keepdims=True) + cq2[live]
        if j == 0:
            m_new = jnp.broadcast_to(m_blk, (seq, LANES))
        else:
            m_prev = m_scr[live, :]
            m_new = jnp.maximum(m_prev, m_blk)
            alpha = jnp.exp2(m_prev - m_new)
        shift = m_new - cq2[live]
        p = jnp.concatenate([jnp.exp2(g - shift).astype(BF16) for g in groups], axis=-1)
        pv = jnp.dot(p, vj, preferred_element_type=F32)
        m_scr[live, :] = m_new
        if j == 0:
            acc_scr[...] = pv
        else:
            acc_scr[live, :] = jnp.concatenate([alpha, alpha], axis=-1) * acc_scr[live, :] + pv

    z = z_ref[...]
    o_ref[...] = ((acc_scr[:, :HEAD_DIM] / acc_scr[:, HEAD_DIM:])
                  * (z * jax.nn.sigmoid(z))).astype(o_ref.dtype)


N_RET_IN, N_FOX_IN, N_RET_SCR = 11, 6, 7


def _mixers_kernel(*refs, tq):
    ret_in, refs = refs[:N_RET_IN], refs[N_RET_IN:]
    fox_in, refs = refs[:N_FOX_IN], refs[N_FOX_IN:]
    (y_ret_ref, y_fox_ref), refs = refs[:2], refs[2:]
    ret_scr, fox_scr = refs[:N_RET_SCR], refs[N_RET_SCR:]
    _fox_body(*fox_in, y_fox_ref, *fox_scr, tq=tq)
    _retention_body(*ret_in, y_ret_ref, *ret_scr)


def _mixers(proj, tables, gn_g, c_col, c_row, batch, seq, tq):
    cos, sin, decay, xi, zeta, gch = tables
    m = proj.shape[0]
    head_block = lambda off: pl.BlockSpec((seq, HEAD_DIM), lambda b, h: (b, off + h))
    per_head = lambda shape: pl.BlockSpec((1,) + shape, lambda b, h: (h, 0, 0))
    table = pl.BlockSpec((seq, HEAD_DIM), lambda b, h: (0, 0))
    y_spec = pl.BlockSpec((seq, HEAD_DIM), lambda b, h: (b, h))
    y_shape = jax.ShapeDtypeStruct((m, WIDTH), BF16)
    return pl.pallas_call(
        functools.partial(_mixers_kernel, tq=tq),
        grid=(batch, N_HEADS),
        in_specs=[
            head_block(COL_RET), head_block(COL_RET + N_HEADS),
            head_block(COL_RET + 2 * N_HEADS), head_block(COL_RET + 3 * N_HEADS),
            table, table,
            per_head((CHUNK, CHUNK)), per_head((CHUNK, HEAD_DIM)),
            per_head((CHUNK, HEAD_DIM)), per_head((8, HEAD_DIM)), per_head((1, HEAD_DIM)),
            head_block(COL_FOX), head_block(COL_FOX + N_HEADS),
            head_block(COL_FOX + 2 * N_HEADS), head_block(COL_FOX + 3 * N_HEADS),
            pl.BlockSpec((seq, LANES), lambda b, h: (b, 0)),
            pl.BlockSpec((1, seq // tq, 1, tq), lambda b, h: (b * N_HEADS + h, 0, 0, 0)),
        ],
        out_specs=[y_spec, y_spec],
        out_shape=[y_shape, y_shape],
        scratch_shapes=([pltpu.VMEM((seq, HEAD_DIM), BF16)] * 6 + [pltpu.VMEM((seq, HEAD_DIM), F32)]
                        + [pltpu.VMEM((seq, LANES), F32), pltpu.VMEM((seq, 2 * HEAD_DIM), F32)]),
        compiler_params=_params(("parallel", "arbitrary")),
        name="mixers",
    )(proj, proj, proj, proj, cos, sin, decay, xi, zeta, gch, gn_g.reshape(N_HEADS, 1, HEAD_DIM),
      proj, proj, proj, proj, c_col, c_row)


POOL_HALO = 16


def _pool_group(gi, u_ref, halo_ref, z_ref, w_ref, s_ref, tile_in_seq):
    tm = u_ref.shape[0]
    w = POOL_WINDOWS[gi]
    cols = slice(gi * POOL_GROUP_DIM, (gi + 1) * POOL_GROUP_DIM)
    halo = jnp.where(tile_in_seq == 0, 0.0, halo_ref[:, cols])
    t1 = (tile_in_seq * tm + lax.broadcasted_iota(jnp.int32, (tm, LANES), 0) + 1).astype(F32)
    u = u_ref[:, cols]
    total, k = jnp.concatenate([halo, u], axis=0), 1
    while k < w:
        total = total + pltpu.roll(total, k, 0)
        k *= 2
    inv = 1.0 / jnp.minimum(t1, float(w))
    mean = jnp.concatenate([x * inv for x in _lane_groups(total[POOL_HALO:])], axis=-1)
    mixed = jnp.dot((mean - u).astype(BF16), w_ref[gi], preferred_element_type=F32) * s_ref[gi]
    z = z_ref[:, cols]
    return (mixed * (z * jax.nn.sigmoid(z))).astype(BF16)


def _merge_kernel(x_ref, yr_ref, yf_ref, u_ref, halo_ref, pz_ref, ga_ref, gb_ref, gc_ref,
                  pw_ref, ps_ref, wr_ref, wf_ref, wp_ref, wo_ref, fg_ref, o_ref,
                  *, tiles_per_seq, final_norm):
    tile_in_seq = pl.program_id(0) % tiles_per_seq
    quarter = o_ref.shape[1] // 4
    pool = functools.partial(_pool_group, u_ref=u_ref, halo_ref=halo_ref, z_ref=pz_ref,
                             w_ref=pw_ref, s_ref=ps_ref, tile_in_seq=tile_in_seq)

    yr = yr_ref[...]
    br, yp = [], []
    for part in range(4):
        cols = slice(part * quarter, (part + 1) * quarter)
        br.append(jnp.dot(yr, wr_ref[:, cols], preferred_element_type=F32))
        yp.append(pool(part))
    merged = (jax.nn.sigmoid(ga_ref[...]) * jnp.concatenate(br, axis=-1)
              + jax.nn.sigmoid(gb_ref[...]) * jnp.dot(yf_ref[...], wf_ref[...], preferred_element_type=F32)
              + jax.nn.sigmoid(gc_ref[...]) * jnp.dot(jnp.concatenate(yp, axis=-1), wp_ref[...],
                                                      preferred_element_type=F32))
    x = x_ref[...] + jnp.dot(merged.astype(BF16), wo_ref[...], preferred_element_type=F32)
    if final_norm:
        x = x * lax.rsqrt(jnp.mean(x * x, axis=-1, keepdims=True) + EPS) * fg_ref[...]
    o_ref[...] = x


def _merge(x2, y_ret, y_fox, proj, pool_w, pool_scale, w_r, w_f, w_p, w_o, final_g,
           layer, seq, final_norm, tm=256):
    m, d = x2.shape
    n_groups = len(POOL_WINDOWS)
    halo_per_tile = tm // POOL_HALO
    row = lambda width: pl.BlockSpec((tm, width), lambda i: (i, 0))
    col = lambda width, off: pl.BlockSpec((tm, width), lambda i: (i, off * LANES // width))
    once = lambda shape, index: pl.BlockSpec(shape, index, pipeline_mode=pl.Buffered(1))
    weight = lambda rows: once((None, rows, d), lambda i: (layer, 0, 0))
    halo = pl.BlockSpec((POOL_HALO, WIDTH),
                        lambda i: (jnp.maximum(i * halo_per_tile - 1, 0), COL_POOL_U * LANES // WIDTH))
    return pl.pallas_call(
        functools.partial(_merge_kernel, tiles_per_seq=seq // tm, final_norm=final_norm),
        grid=(m // tm,),
        in_specs=[
            row(d), row(WIDTH), row(WIDTH),
            col(WIDTH, COL_POOL_U), halo, col(WIDTH, COL_POOL_Z),
            col(d, COL_GATES), col(d, COL_GATES + d // LANES), col(d, COL_GATES + 2 * d // LANES),
            once((None, n_groups, POOL_GROUP_DIM, POOL_GROUP_DIM), lambda i: (layer, 0, 0, 0)),
            once((None, n_groups, 1, POOL_GROUP_DIM), lambda i: (layer, 0, 0, 0)),
            weight(WIDTH), weight(WIDTH), weight(WIDTH), weight(d),
            once((1, d), lambda i: (0, 0)),
        ],
        out_specs=row(d),
        out_shape=jax.ShapeDtypeStruct((m, d), F32),
        compiler_params=_params(("parallel",)),
        name="merge_out",
    )(x2, y_ret, y_fox, proj, proj, proj, proj, proj, proj,
      pool_w, pool_scale, w_r, w_f, w_p, w_o, final_g)


def kernel(x, norm_g, w_in, ret_gn_g, fox_b_f, pool_w, pool_scale,
           w_ret_branch, w_fox_branch, w_pool_branch, w_out, final_g):
    batch, seq, d = x.shape
    depth = w_in.shape[0]
    tables = _retention_tables(seq)
    n_before = 8 * WIDTH
    w_in_t = jnp.swapaxes(w_in, 1, 2)
    w_main = _prep_w_in(w_in_t, 0)
    w_forget = jnp.pad(w_in_t[:, n_before:n_before + N_HEADS, :],
                       ((0, 0), (0, LANES - N_HEADS), (0, 0))).astype(BF16)
    branch_weights = (w_ret_branch, w_fox_branch, w_pool_branch, w_out)
    pool_w_b = pool_w.astype(BF16)
    pool_scale_r = pool_scale.reshape(depth, len(POOL_WINDOWS), 1, POOL_GROUP_DIM)
    x2 = x.reshape(batch * seq, d)
    for layer in range(depth):
        bias = jnp.pad(fox_b_f[layer], (0, LANES - N_HEADS)).reshape(1, LANES)

        to_bf16 = [t.reshape(-1, d) for t in branch_weights] if layer == 0 else ()
        proj, f_logits, w_main, casts = _inproj(x2, norm_g[layer].reshape(1, d), w_main, w_forget,
                                                w_in_t, layer, to_bf16)
        if layer == 0:
            w_r, w_f, w_p, w_o = (c.reshape(t.shape) for c, t in zip(casts, branch_weights))
        c_col, c_row_t = _forget_cumsum(f_logits, bias, batch, seq)
        c_row = c_row_t[:, :N_HEADS, :].reshape(
            batch * N_HEADS, seq // FOX_BLOCK, 1, FOX_BLOCK)
        y_ret, y_fox = _mixers(proj, tables, ret_gn_g[layer], c_col, c_row, batch, seq, FOX_BLOCK)
        x2 = _merge(x2, y_ret, y_fox, proj, pool_w_b, pool_scale_r, w_r, w_f, w_p, w_o,
                    final_g.reshape(1, d), layer, seq, final_norm=(layer == depth - 1))
    return x2.reshape(batch, seq, d)
```

```python
import functools

import jax
import jax.numpy as jnp
from jax import lax
from jax.experimental import pallas as pl
from jax.experimental.pallas import tpu as pltpu

D_MODEL = 2048
HEAD_DIM = 128
N_HEADS = 8
WIDTH = D_MODEL // 2
POOL_WINDOWS = (2, 4, 8, 16)
POOL_GROUP_DIM = WIDTH // len(POOL_WINDOWS)
CHUNK = 128
ROPE_THETA = 10000.0
EPS = 1e-6
NEG_INF = -1e30
LOG2E = 1.4426950408889634
FOX_BLOCK = 512

LANES = 128
N_MAIN = 10 * WIDTH + 3 * D_MODEL
COL_RET = 0
COL_FOX = 4 * N_HEADS
COL_POOL_U = 8 * N_HEADS
COL_POOL_Z = 9 * N_HEADS
COL_GATES = 10 * N_HEADS

BF16 = jnp.bfloat16
F32 = jnp.float32
NT_DIMS = (((1,), (1,)), ((), ()))

VMEM_LIMIT = 56 * 1024 * 1024


def _params(sem, vmem=VMEM_LIMIT):
    return pltpu.CompilerParams(dimension_semantics=sem, vmem_limit_bytes=vmem)


def _inproj_kernel(x_hbm, g_ref, w_ref, wf_ref, *rest, n_casts):
    cast_srcs, rest = rest[:n_casts], rest[n_casts:]
    (o_ref, f_ref), rest = rest[:2], rest[2:]
    cast_dsts, (x_buf, h_scr, x_sem) = rest[:n_casts], rest[n_casts:]
    i, j = pl.program_id(0), pl.program_id(1)
    tm = x_buf.shape[0]

    def x_copy(tile):
        return pltpu.make_async_copy(x_hbm.at[pl.ds(tile * tm, tm), :], x_buf, x_sem)

    @pl.when((i == 0) & (j == 0))
    def _():
        x_copy(0).start()

    @pl.when(j == 0)
    def _():
        x_copy(i).wait()
        x = x_buf[...]
        y = x * lax.rsqrt(jnp.mean(x * x, axis=-1, keepdims=True) + EPS)
        h = (y * g_ref[...]).astype(BF16)
        h_scr[...] = h
        f_ref[...] = lax.dot_general(h, wf_ref[...], NT_DIMS, preferred_element_type=F32)

    @pl.when((j == 1) & (i + 1 < pl.num_programs(0)))
    def _():
        x_copy(i + 1).start()

    h = h_scr[...]
    half = o_ref.shape[1] // 2
    o_ref[:, :half] = lax.dot_general(h, w_ref[:half, :], NT_DIMS, preferred_element_type=F32)
    for src, dst in zip(cast_srcs, cast_dsts):
        dst[...] = src[...].reshape(dst.shape).astype(BF16)
    o_ref[:, half:] = lax.dot_general(h, w_ref[half:, :], NT_DIMS, preferred_element_type=F32)


def _w_rows_spec(layer, block_of):
    blocks_per_section = 4 * N_HEADS

    def index(*ids):
        r = block_of(*ids)
        within = r % blocks_per_section
        src = (r // blocks_per_section) * blocks_per_section + (within % 4) * N_HEADS + within // 4
        src = jnp.where(r < 2 * blocks_per_section, src, r)
        gap = jnp.where(r < 2 * blocks_per_section, 0, N_HEADS)
        return layer, pl.multiple_of(src * CHUNK + gap, N_HEADS), 0
    return pl.BlockSpec((pl.Element(1), pl.Element(CHUNK), pl.Element(D_MODEL)), index)


def _wprep_kernel(*refs):
    *srcs, out_ref = refs
    for q, src in enumerate(srcs):
        out_ref[q * CHUNK:(q + 1) * CHUNK, :] = src[0].astype(BF16)


def _prep_w_in(w_in_t, layer, blocks_per_step=4):
    d = w_in_t.shape[2]
    tr = blocks_per_step * CHUNK
    return pl.pallas_call(
        _wprep_kernel,
        grid=(N_MAIN // tr,),
        in_specs=[_w_rows_spec(layer, lambda i, q=q: i * blocks_per_step + q)
                  for q in range(blocks_per_step)],
        out_specs=pl.BlockSpec((tr, d), lambda i: (i, 0)),
        out_shape=jax.ShapeDtypeStruct((N_MAIN, d), BF16),
        compiler_params=_params(("parallel",)),
        name="w_in_prep",
    )(*([w_in_t] * blocks_per_step))


def _inproj(x2, g, w_main_t, w_f_t, w_in_t, layer, to_bf16=(), tm=1024, tn=2048):
    m, d = x2.shape
    n = w_main_t.shape[0]
    grid = (m // tm, n // tn)
    n_steps = grid[0] * grid[1]
    step = lambda i, j: i * grid[1] + j
    prep_next = layer + 1 < w_in_t.shape[0]
    in_specs = [
        pl.BlockSpec(memory_space=pl.ANY),
        pl.BlockSpec((1, d), lambda i, j: (0, 0)),
        pl.BlockSpec((tn, d), lambda i, j: (j, 0)),
        pl.BlockSpec((None, LANES, d), lambda i, j: (layer, 0, 0)),
    ]
    out_specs = [
        pl.BlockSpec((tm, tn), lambda i, j: (i, j)),
        pl.BlockSpec((tm, LANES), lambda i, j: (i, 0)),
    ]
    out_shape = [jax.ShapeDtypeStruct((m, n), F32), jax.ShapeDtypeStruct((m, LANES), F32)]
    operands = [x2, g, w_main_t, w_f_t]
    row_block = lambda rows: pl.BlockSpec((rows, d), lambda i, j: (step(i, j), 0))
    if prep_next:
        assert N_MAIN == n_steps * CHUNK, "one 128-row weight block per grid step"
        in_specs.append(_w_rows_spec(layer + 1, step))
        out_specs.append(row_block(CHUNK))
        out_shape.append(jax.ShapeDtypeStruct((N_MAIN, d), BF16))
        operands.append(w_in_t)
    for t in to_bf16:
        in_specs.append(row_block(t.shape[0] // n_steps))
        out_specs.append(row_block(t.shape[0] // n_steps))
        out_shape.append(jax.ShapeDtypeStruct(t.shape, BF16))
        operands.append(t)
    outs = pl.pallas_call(
        functools.partial(_inproj_kernel, n_casts=len(operands) - 4),
        grid=grid,
        in_specs=in_specs,
        out_specs=out_specs,
        out_shape=out_shape,
        scratch_shapes=[pltpu.VMEM((tm, d), F32), pltpu.VMEM((tm, d), BF16),
                        pltpu.SemaphoreType.DMA(())],
        compiler_params=_params(("arbitrary", "arbitrary")),
        name="inproj",
    )(*operands)
    proj, f_logits, *casts = outs
    w_next = casts.pop(0) if prep_next else None
    return proj, f_logits, w_next, casts


def _rotate(t, cos, sin_signed):
    return t * cos + pltpu.roll(t, HEAD_DIM // 2, 1) * sin_signed


def _retention_body(q_ref, k_ref, v_ref, z_ref, cos_ref, sin_ref, decay_ref,
                    xi_ref, zeta_ref, gch_ref, gn_ref, o_ref,
                    q_scr, k_scr, kz_scr, v_scr, inner_scr, state_scr, out_scr):
    decay = decay_ref[0]
    xi = xi_ref[0]
    zeta = zeta_ref[0]
    g_chunk = gch_ref[0, 0:1, :]
    n_chunks = q_ref.shape[0] // CHUNK
    chunk_rows = [pl.ds(c * CHUNK, CHUNK) for c in range(n_chunks)]
    nt = (((1,), (1,)), ((), ()))
    tn = (((0,), (0,)), ((), ()))

    cos = cos_ref[...]
    sin = sin_ref[...]
    q_scr[...] = _rotate(q_ref[...], cos, sin).astype(BF16)
    k = _rotate(k_ref[...], cos, sin) * (HEAD_DIM ** -0.5)
    k_scr[...] = k.astype(BF16)
    v_scr[...] = v_ref[...].astype(BF16)
    for c, rows in enumerate(chunk_rows):
        kz_scr[rows, :] = (k[c * CHUNK:(c + 1) * CHUNK] * zeta).astype(BF16)

    increments = [lax.dot_general(kz_scr[rows, :], v_scr[rows, :], tn, preferred_element_type=F32)
                  for rows in chunk_rows]
    for rows in chunk_rows:
        inner = lax.dot_general(q_scr[rows, :], k_scr[rows, :], nt,
                                preferred_element_type=F32) * decay
        inner_scr[rows, :] = inner.astype(BF16)

    state = jnp.zeros((HEAD_DIM, HEAD_DIM), F32)
    for rows, inc in zip(chunk_rows, increments):
        state_scr[rows, :] = state.astype(BF16)
        state = g_chunk * state + inc

    for rows in chunk_rows:
        out_scr[rows, :] = (
            jnp.dot(inner_scr[rows, :], v_scr[rows, :], preferred_element_type=F32)
            + jnp.dot(q_scr[rows, :], state_scr[rows, :], preferred_element_type=F32) * xi)

    out = out_scr[...]
    mu = jnp.mean(out, axis=-1, keepdims=True)
    dev = out - mu
    var = jnp.mean(dev * dev, axis=-1, keepdims=True)
    yn = dev * lax.rsqrt(var + EPS) * gn_ref[0]
    z = z_ref[...]
    o_ref[...] = (yn * (z * jax.nn.sigmoid(z))).astype(o_ref.dtype)


def _retention_tables(seq):
    half = HEAD_DIM // 2
    inv = ROPE_THETA ** (-jnp.arange(half, dtype=F32) / half)
    ang = jnp.arange(seq, dtype=F32)[:, None] * inv[None, :]
    cos = jnp.cos(ang)
    sin = jnp.sin(ang)
    cos2 = jnp.concatenate([cos, cos], axis=-1)
    sin2 = jnp.concatenate([-sin, sin], axis=-1)
    log_g = jnp.log1p(-jnp.exp2(-5.0 - jnp.arange(N_HEADS, dtype=F32)))
    idx = jnp.arange(CHUNK, dtype=F32)
    diff = idx[:, None] - idx[None, :]
    decay = jnp.where(diff >= 0, jnp.exp(log_g[:, None, None] * jnp.maximum(diff, 0.0)), 0.0)
    xi = jnp.exp(log_g[:, None] * (idx + 1.0))
    zeta = jnp.exp(log_g[:, None] * (CHUNK - 1.0 - idx))
    g_chunk = jnp.exp(log_g * CHUNK)
    bcast = lambda t: jnp.broadcast_to(t[:, :, None], (N_HEADS, CHUNK, HEAD_DIM))
    gch = jnp.broadcast_to(g_chunk[:, None, None], (N_HEADS, 8, HEAD_DIM))
    return cos2, sin2, decay, bcast(xi), bcast(zeta), gch


def _split3(x):
    hi = x.astype(BF16)
    r1 = x - hi.astype(F32)
    mid = r1.astype(BF16)
    lo = (r1 - mid.astype(F32)).astype(BF16)
    return hi, mid, lo


def _forget_cumsum_kernel(f_ref, b_ref, col_ref, row_ref):
    n_blocks = f_ref.shape[0] // CHUNK
    r = lax.broadcasted_iota(jnp.int32, (CHUNK, CHUNK), 0)
    c = lax.broadcasted_iota(jnp.int32, (CHUNK, CHUNK), 1)
    tri = jnp.where(r >= c, 1.0, 0.0).astype(BF16)
    carry = jnp.zeros((1, LANES), F32)
    for blk in range(n_blocks):
        rows = pl.ds(blk * CHUNK, CHUNK)
        x = f_ref[rows, :] + b_ref[...]
        ls = jnp.minimum(x, 0.0) - jnp.log1p(jnp.exp(-jnp.abs(x)))
        hi, mid, lo = _split3(ls)
        cs = (jnp.dot(tri, hi, preferred_element_type=F32)
              + jnp.dot(tri, mid, preferred_element_type=F32)
              + jnp.dot(tri, lo, preferred_element_type=F32)) + carry
        col_ref[rows, :] = cs
        row_ref[0, :, rows] = cs.T
        carry = cs[CHUNK - 1:CHUNK, :]


def _forget_cumsum(f_logits, bias, batch, seq):
    m = f_logits.shape[0]
    return pl.pallas_call(
        _forget_cumsum_kernel,
        grid=(batch,),
        in_specs=[
            pl.BlockSpec((seq, LANES), lambda b: (b, 0)),
            pl.BlockSpec((1, LANES), lambda b: (0, 0)),
        ],
        out_specs=[
            pl.BlockSpec((seq, LANES), lambda b: (b, 0)),
            pl.BlockSpec((1, LANES, seq), lambda b: (b, 0, 0)),
        ],
        out_shape=[
            jax.ShapeDtypeStruct((m, LANES), F32),
            jax.ShapeDtypeStruct((batch, LANES, seq), F32),
        ],
        compiler_params=_params(("parallel",)),
        name="forget_cumsum",
    )(f_logits, bias)


def _lane_groups(x):
    return [x[:, g * LANES:(g + 1) * LANES] for g in range(x.shape[1] // LANES)]


def _fox_body(q_ref, k_ref, v_ref, z_ref, ccol_ref, crow_ref, o_ref,
              m_scr, acc_scr, *, tq):
    h = pl.program_id(1)
    seq = q_ref.shape[0]
    c1 = (HEAD_DIM ** -0.5) * LOG2E
    ones = jnp.ones((tq, HEAD_DIM), BF16)

    qb = q_ref[...].astype(BF16)
    lane = lax.broadcasted_iota(jnp.int32, (seq, LANES), 1)
    cq2 = jnp.sum(jnp.where(lane == h, ccol_ref[...], 0.0), axis=-1, keepdims=True) * LOG2E
    causal = (lax.broadcasted_iota(jnp.int32, (tq, tq), 0)
              >= lax.broadcasted_iota(jnp.int32, (tq, tq), 1))

    for j in range(seq // tq):
        lo = j * tq
        live = slice(lo, seq)
        kj = k_ref[lo:lo + tq, :].astype(BF16)
        vj = jnp.concatenate([v_ref[lo:lo + tq, :].astype(BF16), ones], axis=-1)
        s = lax.dot_general(qb[live], kj, (((1,), (1,)), ((), ())), preferred_element_type=F32)
        a = s * c1 - crow_ref[0, j] * LOG2E
        diag = jnp.where(causal, a[:tq], NEG_INF)
        a = diag if lo + tq == seq else jnp.concatenate([diag, a[tq:]], axis=0)
        groups = _lane_groups(a)
        m_blk = jnp.max(functools.reduce(jnp.maximum, groups), axis=-1, keepdims=True) + cq2[live]
        if j == 0:
            m_new = jnp.broadcast_to(m_blk, (seq, LANES))
        else:
            m_prev = m_scr[live, :]
            m_new = jnp.maximum(m_prev, m_blk)
            alpha = jnp.exp2(m_prev - m_new)
        shift = m_new - cq2[live]
        p = jnp.concatenate([jnp.exp2(g - shift).astype(BF16) for g in groups], axis=-1)
        pv = jnp.dot(p, vj, preferred_element_type=F32)
        m_scr[live, :] = m_new
        if j == 0:
            acc_scr[...] = pv
        else:
            acc_scr[live, :] = jnp.concatenate([alpha, alpha], axis=-1) * acc_scr[live, :] + pv

    z = z_ref[...]
    o_ref[...] = ((acc_scr[:, :HEAD_DIM] / acc_scr[:, HEAD_DIM:])
                  * (z * jax.nn.sigmoid(z))).astype(o_ref.dtype)


N_RET_TABLES, N_RET_SCR = 7, 7


def _mixers_kernel(ret_ref, *refs, tq):
    qkvz = lambda ref: [ref.at[:, t * HEAD_DIM:(t + 1) * HEAD_DIM] for t in range(4)]
    ret_tables, refs = refs[:N_RET_TABLES], refs[N_RET_TABLES:]
    (fox_ref, ccol_ref, crow_ref, y_ret_ref, y_fox_ref), refs = refs[:5], refs[5:]
    ret_scr, fox_scr = refs[:N_RET_SCR], refs[N_RET_SCR:]
    _fox_body(*qkvz(fox_ref), ccol_ref, crow_ref, y_fox_ref, *fox_scr, tq=tq)
    _retention_body(*qkvz(ret_ref), *ret_tables, y_ret_ref, *ret_scr)


def _mixers(proj, tables, gn_g, c_col, c_row, batch, seq, tq):
    cos, sin, decay, xi, zeta, gch = tables
    m = proj.shape[0]
    head_block = lambda off: pl.BlockSpec((seq, 4 * HEAD_DIM), lambda b, h: (b, off // 4 + h))
    per_head = lambda shape: pl.BlockSpec((1,) + shape, lambda b, h: (h, 0, 0))
    table = pl.BlockSpec((seq, HEAD_DIM), lambda b, h: (0, 0))
    y_spec = pl.BlockSpec((seq, HEAD_DIM), lambda b, h: (b, h))
    y_shape = jax.ShapeDtypeStruct((m, WIDTH), BF16)
    return pl.pallas_call(
        functools.partial(_mixers_kernel, tq=tq),
        grid=(batch, N_HEADS),
        in_specs=[
            head_block(COL_RET),
            table, table,
            per_head((CHUNK, CHUNK)), per_head((CHUNK, HEAD_DIM)),
            per_head((CHUNK, HEAD_DIM)), per_head((8, HEAD_DIM)), per_head((1, HEAD_DIM)),
            head_block(COL_FOX),
            pl.BlockSpec((seq, LANES), lambda b, h: (b, 0)),
            pl.BlockSpec((1, seq // tq, 1, tq), lambda b, h: (b * N_HEADS + h, 0, 0, 0)),
        ],
        out_specs=[y_spec, y_spec],
        out_shape=[y_shape, y_shape],
        scratch_shapes=([pltpu.VMEM((seq, HEAD_DIM), BF16)] * 6 + [pltpu.VMEM((seq, HEAD_DIM), F32)]
                        + [pltpu.VMEM((seq, LANES), F32), pltpu.VMEM((seq, 2 * HEAD_DIM), F32)]),
        compiler_params=_params(("parallel", "arbitrary")),
        name="mixers",
    )(proj, cos, sin, decay, xi, zeta, gch, gn_g.reshape(N_HEADS, 1, HEAD_DIM),
      proj, c_col, c_row)


POOL_HALO = 16


def _pool_group(gi, u_ref, halo_ref, z_ref, w_ref, s_ref, tile_in_seq):
    tm = u_ref.shape[0]
    w = POOL_WINDOWS[gi]
    cols = slice(gi * POOL_GROUP_DIM, (gi + 1) * POOL_GROUP_DIM)
    halo = jnp.where(tile_in_seq == 0, 0.0, halo_ref[:, cols])
    t1 = (tile_in_seq * tm + lax.broadcasted_iota(jnp.int32, (tm, LANES), 0) + 1).astype(F32)
    u = u_ref[:, cols]
    total, k = jnp.concatenate([halo, u], axis=0), 1
    while k < w:
        total = total + pltpu.roll(total, k, 0)
        k *= 2
    inv = 1.0 / jnp.minimum(t1, float(w))
    mean = jnp.concatenate([x * inv for x in _lane_groups(total[POOL_HALO:])], axis=-1)
    mixed = jnp.dot((mean - u).astype(BF16), w_ref[gi], preferred_element_type=F32) * s_ref[gi]
    z = z_ref[:, cols]
    return (mixed * (z * jax.nn.sigmoid(z))).astype(BF16)


def _merge_kernel(x_ref, yr_ref, yf_ref, u_ref, halo_ref, pz_ref, ga_ref, gb_ref, gc_ref,
                  pw_ref, ps_ref, wr_ref, wf_ref, wp_ref, wo_ref, fg_ref, o_ref,
                  *, tiles_per_seq, final_norm):
    tile_in_seq = pl.program_id(0) % tiles_per_seq
    quarter = o_ref.shape[1] // 4
    pool = functools.partial(_pool_group, u_ref=u_ref, halo_ref=halo_ref, z_ref=pz_ref,
                             w_ref=pw_ref, s_ref=ps_ref, tile_in_seq=tile_in_seq)

    yr = yr_ref[...]
    br, yp = [], []
    for part in range(4):
        cols = slice(part * quarter, (part + 1) * quarter)
        br.append(jnp.dot(yr, wr_ref[:, cols], preferred_element_type=F32))
        yp.append(pool(part))
    merged = (jax.nn.sigmoid(ga_ref[...]) * jnp.concatenate(br, axis=-1)
              + jax.nn.sigmoid(gb_ref[...]) * jnp.dot(yf_ref[...], wf_ref[...], preferred_element_type=F32)
              + jax.nn.sigmoid(gc_ref[...]) * jnp.dot(jnp.concatenate(yp, axis=-1), wp_ref[...],
                                                      preferred_element_type=F32))
    x = x_ref[...] + jnp.dot(merged.astype(BF16), wo_ref[...], preferred_element_type=F32)
    if final_norm:
        x = x * lax.rsqrt(jnp.mean(x * x, axis=-1, keepdims=True) + EPS) * fg_ref[...]
    o_ref[...] = x


def _merge(x2, y_ret, y_fox, proj, pool_w, pool_scale, w_r, w_f, w_p, w_o, final_g,
           layer, seq, final_norm, tm=256):
    m, d = x2.shape
    n_groups = len(POOL_WINDOWS)
    halo_per_tile = tm // POOL_HALO
    row = lambda width: pl.BlockSpec((tm, width), lambda i: (i, 0))
    col = lambda width, off: pl.BlockSpec((tm, width), lambda i: (i, off * LANES // width))
    once = lambda shape, index: pl.BlockSpec(shape, index, pipeline_mode=pl.Buffered(1))
    weight = lambda rows: once((None, rows, d), lambda i: (layer, 0, 0))
    halo = pl.BlockSpec((POOL_HALO, WIDTH),
                        lambda i: (jnp.maximum(i * halo_per_tile - 1, 0), COL_POOL_U * LANES // WIDTH))
    return pl.pallas_call(
        functools.partial(_merge_kernel, tiles_per_seq=seq // tm, final_norm=final_norm),
        grid=(m // tm,),
        in_specs=[
            row(d), row(WIDTH), row(WIDTH),
            col(WIDTH, COL_POOL_U), halo, col(WIDTH, COL_POOL_Z),
            col(d, COL_GATES), col(d, COL_GATES + d // LANES), col(d, COL_GATES + 2 * d // LANES),
            once((None, n_groups, POOL_GROUP_DIM, POOL_GROUP_DIM), lambda i: (layer, 0, 0, 0)),
            once((None, n_groups, 1, POOL_GROUP_DIM), lambda i: (layer, 0, 0, 0)),
            weight(WIDTH), weight(WIDTH), weight(WIDTH), weight(d),
            once((1, d), lambda i: (0, 0)),
        ],
        out_specs=row(d),
        out_shape=jax.ShapeDtypeStruct((m, d), F32),
        compiler_params=_params(("parallel",)),
        name="merge_out",
    )(x2, y_ret, y_fox, proj, proj, proj, proj, proj, proj,
      pool_w, pool_scale, w_r, w_f, w_p, w_o, final_g)


def kernel(x, norm_g, w_in, ret_gn_g, fox_b_f, pool_w, pool_scale,
           w_ret_branch, w_fox_branch, w_pool_branch, w_out, final_g):
    batch, seq, d = x.shape
    depth = w_in.shape[0]
    tables = _retention_tables(seq)
    n_before = 8 * WIDTH
    w_in_t = jnp.swapaxes(w_in, 1, 2)
    w_main = _prep_w_in(w_in_t, 0)
    w_forget = jnp.pad(w_in_t[:, n_before:n_before + N_HEADS, :],
                       ((0, 0), (0, LANES - N_HEADS), (0, 0))).astype(BF16)
    branch_weights = (w_ret_branch, w_fox_branch, w_pool_branch, w_out)
    pool_w_b = pool_w.astype(BF16)
    pool_scale_r = pool_scale.reshape(depth, len(POOL_WINDOWS), 1, POOL_GROUP_DIM)
    x2 = x.reshape(batch * seq, d)
    for layer in range(depth):
        bias = jnp.pad(fox_b_f[layer], (0, LANES - N_HEADS)).reshape(1, LANES)

        to_bf16 = [t.reshape(-1, d) for t in branch_weights] if layer == 0 else ()
        proj, f_logits, w_main, casts = _inproj(x2, norm_g[layer].reshape(1, d), w_main, w_forget,
                                                w_in_t, layer, to_bf16)
        if layer == 0:
            w_r, w_f, w_p, w_o = (c.reshape(t.shape) for c, t in zip(casts, branch_weights))
        c_col, c_row_t = _forget_cumsum(f_logits, bias, batch, seq)
        c_row = c_row_t[:, :N_HEADS, :].reshape(
            batch * N_HEADS, seq // FOX_BLOCK, 1, FOX_BLOCK)
        y_ret, y_fox = _mixers(proj, tables, ret_gn_g[layer], c_col, c_row, batch, seq, FOX_BLOCK)
        x2 = _merge(x2, y_ret, y_fox, proj, pool_w_b, pool_scale_r, w_r, w_f, w_p, w_o,
                    final_g.reshape(1, d), layer, seq, final_norm=(layer == depth - 1))
    return x2.reshape(batch, seq, d)
```

```python
import functools

import jax
import jax.numpy as jnp
from jax import lax
from jax.experimental import pallas as pl
from jax.experimental.pallas import tpu as pltpu

D_MODEL = 2048
HEAD_DIM = 128
N_HEADS = 8
WIDTH = D_MODEL // 2
POOL_WINDOWS = (2, 4, 8, 16)
POOL_GROUP_DIM = WIDTH // len(POOL_WINDOWS)
CHUNK = 128
ROPE_THETA = 10000.0
EPS = 1e-6
NEG_INF = -1e30
LOG2E = 1.4426950408889634
FOX_BLOCK = 512

LANES = 128
N_MAIN = 10 * WIDTH + 3 * D_MODEL
COL_RET = 0
COL_FOX = 4 * N_HEADS
COL_POOL_U = 8 * N_HEADS
COL_POOL_Z = 9 * N_HEADS
COL_GATES = 10 * N_HEADS

BF16 = jnp.bfloat16
F32 = jnp.float32
NT_DIMS = (((1,), (1,)), ((), ()))

VMEM_LIMIT = 56 * 1024 * 1024


def _params(sem, vmem=VMEM_LIMIT):
    return pltpu.CompilerParams(dimension_semantics=sem, vmem_limit_bytes=vmem)


def _inproj_kernel(x_hbm, g_ref, w_ref, wf_ref, *rest, n_casts):
    cast_srcs, rest = rest[:n_casts], rest[n_casts:]
    (o_ref, f_ref), rest = rest[:2], rest[2:]
    cast_dsts, (x_buf, h_scr, x_sem) = rest[:n_casts], rest[n_casts:]
    i, j = pl.program_id(0), pl.program_id(1)
    tm = x_buf.shape[0]

    def x_copy(tile):
        return pltpu.make_async_copy(x_hbm.at[pl.ds(tile * tm, tm), :], x_buf, x_sem)

    @pl.when((i == 0) & (j == 0))
    def _():
        x_copy(0).start()

    @pl.when(j == 0)
    def _():
        x_copy(i).wait()
        x = x_buf[...]
        y = x * lax.rsqrt(jnp.mean(x * x, axis=-1, keepdims=True) + EPS)
        h = (y * g_ref[...]).astype(BF16)
        h_scr[...] = h
        f_ref[...] = lax.dot_general(h, wf_ref[...], NT_DIMS, preferred_element_type=F32)

    @pl.when((j == 1) & (i + 1 < pl.num_programs(0)))
    def _():
        x_copy(i + 1).start()

    h = h_scr[...]
    half = o_ref.shape[1] // 2
    o_ref[:, :half] = lax.dot_general(h, w_ref[:half, :], NT_DIMS, preferred_element_type=F32)
    for src, dst in zip(cast_srcs, cast_dsts):
        dst[...] = src[...].reshape(dst.shape).astype(BF16)
    o_ref[:, half:] = lax.dot_general(h, w_ref[half:, :], NT_DIMS, preferred_element_type=F32)


def _w_rows_spec(layer, rows, step_of):
    def index(*ids):
        first = step_of(*ids) * rows
        return layer, pl.multiple_of(first + jnp.where(first >= 8 * WIDTH, N_HEADS, 0), N_HEADS), 0
    return pl.BlockSpec((pl.Element(1), pl.Element(rows), pl.Element(D_MODEL)), index)


def _wprep_kernel(src_ref, out_ref):
    out_ref[...] = src_ref[0].astype(BF16)


def _prep_w_in(w_in_t, layer, tr=512):
    d = w_in_t.shape[2]
    return pl.pallas_call(
        _wprep_kernel,
        grid=(N_MAIN // tr,),
        in_specs=[_w_rows_spec(layer, tr, lambda i: i)],
        out_specs=pl.BlockSpec((tr, d), lambda i: (i, 0)),
        out_shape=jax.ShapeDtypeStruct((N_MAIN, d), BF16),
        compiler_params=_params(("parallel",)),
        name="w_in_prep",
    )(w_in_t)


def _inproj(x2, g, w_main_t, w_f_t, w_in_t, layer, to_bf16=(), tm=1024, tn=2048):
    m, d = x2.shape
    n = w_main_t.shape[0]
    grid = (m // tm, n // tn)
    n_steps = grid[0] * grid[1]
    step = lambda i, j: i * grid[1] + j
    prep_next = layer + 1 < w_in_t.shape[0]
    in_specs = [
        pl.BlockSpec(memory_space=pl.ANY),
        pl.BlockSpec((1, d), lambda i, j: (0, 0)),
        pl.BlockSpec((tn, d), lambda i, j: (j, 0)),
        pl.BlockSpec((None, LANES, d), lambda i, j: (layer, 0, 0)),
    ]
    out_specs = [
        pl.BlockSpec((tm, tn), lambda i, j: (i, j)),
        pl.BlockSpec((tm, LANES), lambda i, j: (i, 0)),
    ]
    out_shape = [jax.ShapeDtypeStruct((m, n), F32), jax.ShapeDtypeStruct((m, LANES), F32)]
    operands = [x2, g, w_main_t, w_f_t]
    row_block = lambda rows: pl.BlockSpec((rows, d), lambda i, j: (step(i, j), 0))
    if prep_next:
        in_specs.append(_w_rows_spec(layer + 1, N_MAIN // n_steps, step))
        out_specs.append(row_block(N_MAIN // n_steps))
        out_shape.append(jax.ShapeDtypeStruct((N_MAIN, d), BF16))
        operands.append(w_in_t)
    for t in to_bf16:
        in_specs.append(row_block(t.shape[0] // n_steps))
        out_specs.append(row_block(t.shape[0] // n_steps))
        out_shape.append(jax.ShapeDtypeStruct(t.shape, BF16))
        operands.append(t)
    outs = pl.pallas_call(
        functools.partial(_inproj_kernel, n_casts=len(operands) - 4),
        grid=grid,
        in_specs=in_specs,
        out_specs=out_specs,
        out_shape=out_shape,
        scratch_shapes=[pltpu.VMEM((tm, d), F32), pltpu.VMEM((tm, d), BF16),
                        pltpu.SemaphoreType.DMA(())],
        compiler_params=_params(("arbitrary", "arbitrary")),
        name="inproj",
    )(*operands)
    proj, f_logits, *casts = outs
    w_next = casts.pop(0) if prep_next else None
    return proj, f_logits, w_next, casts


def _rotate(t, cos, sin_signed):
    return t * cos + pltpu.roll(t, HEAD_DIM // 2, 1) * sin_signed


def _retention_body(q_ref, k_ref, v_ref, z_ref, cos_ref, sin_ref, decay_ref,
                    xi_ref, zeta_ref, gch_ref, gn_ref, o_ref,
                    q_scr, k_scr, kz_scr, v_scr, inner_scr, state_scr, out_scr):
    decay = decay_ref[0]
    xi = xi_ref[0]
    zeta = zeta_ref[0]
    g_chunk = gch_ref[0, 0:1, :]
    n_chunks = q_ref.shape[0] // CHUNK
    chunk_rows = [pl.ds(c * CHUNK, CHUNK) for c in range(n_chunks)]
    nt = (((1,), (1,)), ((), ()))
    tn = (((0,), (0,)), ((), ()))

    cos = cos_ref[...]
    sin = sin_ref[...]
    q_scr[...] = _rotate(q_ref[...], cos, sin).astype(BF16)
    k = _rotate(k_ref[...], cos, sin) * (HEAD_DIM ** -0.5)
    k_scr[...] = k.astype(BF16)
    v_scr[...] = v_ref[...].astype(BF16)
    for c, rows in enumerate(chunk_rows):
        kz_scr[rows, :] = (k[c * CHUNK:(c + 1) * CHUNK] * zeta).astype(BF16)
    yield

    increments = [lax.dot_general(kz_scr[rows, :], v_scr[rows, :], tn, preferred_element_type=F32)
                  for rows in chunk_rows]
    for rows in chunk_rows:
        inner = lax.dot_general(q_scr[rows, :], k_scr[rows, :], nt,
                                preferred_element_type=F32) * decay
        inner_scr[rows, :] = inner.astype(BF16)

    state = jnp.zeros((HEAD_DIM, HEAD_DIM), F32)
    for rows, inc in zip(chunk_rows, increments):
        state_scr[rows, :] = state.astype(BF16)
        state = g_chunk * state + inc

    for rows in chunk_rows:
        out_scr[rows, :] = (
            jnp.dot(inner_scr[rows, :], v_scr[rows, :], preferred_element_type=F32)
            + jnp.dot(q_scr[rows, :], state_scr[rows, :], preferred_element_type=F32) * xi)

    out = out_scr[...]
    mu = jnp.mean(out, axis=-1, keepdims=True)
    dev = out - mu
    var = jnp.mean(dev * dev, axis=-1, keepdims=True)
    yn = dev * lax.rsqrt(var + EPS) * gn_ref[0]
    z = z_ref[...]
    o_ref[...] = (yn * (z * jax.nn.sigmoid(z))).astype(o_ref.dtype)


def _retention_tables(seq):
    half = HEAD_DIM // 2
    inv = ROPE_THETA ** (-jnp.arange(half, dtype=F32) / half)
    ang = jnp.arange(seq, dtype=F32)[:, None] * inv[None, :]
    cos = jnp.cos(ang)
    sin = jnp.sin(ang)
    cos2 = jnp.concatenate([cos, cos], axis=-1)
    sin2 = jnp.concatenate([-sin, sin], axis=-1)
    log_g = jnp.log1p(-jnp.exp2(-5.0 - jnp.arange(N_HEADS, dtype=F32)))
    idx = jnp.arange(CHUNK, dtype=F32)
    diff = idx[:, None] - idx[None, :]
    decay = jnp.where(diff >= 0, jnp.exp(log_g[:, None, None] * jnp.maximum(diff, 0.0)), 0.0)
    xi = jnp.exp(log_g[:, None] * (idx + 1.0))
    zeta = jnp.exp(log_g[:, None] * (CHUNK - 1.0 - idx))
    g_chunk = jnp.exp(log_g * CHUNK)
    bcast = lambda t: jnp.broadcast_to(t[:, :, None], (N_HEADS, CHUNK, HEAD_DIM))
    gch = jnp.broadcast_to(g_chunk[:, None, None], (N_HEADS, 8, HEAD_DIM))
    return cos2, sin2, decay, bcast(xi), bcast(zeta), gch


def _split3(x):
    hi = x.astype(BF16)
    r1 = x - hi.astype(F32)
    mid = r1.astype(BF16)
    lo = (r1 - mid.astype(F32)).astype(BF16)
    return hi, mid, lo


def _forget_cumsum_body(f_ref, b_ref, col_ref, row_ref):
    n_blocks = f_ref.shape[0] // CHUNK
    r = lax.broadcasted_iota(jnp.int32, (CHUNK, CHUNK), 0)
    c = lax.broadcasted_iota(jnp.int32, (CHUNK, CHUNK), 1)
    tri = jnp.where(r >= c, 1.0, 0.0).astype(BF16)
    carry = jnp.zeros((1, LANES), F32)
    for blk in range(n_blocks):
        rows = pl.ds(blk * CHUNK, CHUNK)
        x = f_ref[rows, :] + b_ref[...]
        ls = jnp.minimum(x, 0.0) - jnp.log1p(jnp.exp(-jnp.abs(x)))
        hi, mid, lo = _split3(ls)
        cs = (jnp.dot(tri, hi, preferred_element_type=F32)
              + jnp.dot(tri, mid, preferred_element_type=F32)
              + jnp.dot(tri, lo, preferred_element_type=F32)) + carry
        col_ref[rows, :] = cs
        row_ref[:, rows] = cs.T
        carry = cs[CHUNK - 1:CHUNK, :]


def _lane_groups(x):
    return [x[:, g * LANES:(g + 1) * LANES] for g in range(x.shape[1] // LANES)]


def _fox_body(q_ref, k_ref, v_ref, z_ref, ccol_ref, crow_ref, o_ref,
              m_scr, acc_scr, *, tq):
    h = pl.program_id(1)
    seq = q_ref.shape[0]
    c1 = (HEAD_DIM ** -0.5) * LOG2E
    ones = jnp.ones((tq, HEAD_DIM), BF16)

    qb = q_ref[...].astype(BF16)
    lane = lax.broadcasted_iota(jnp.int32, (seq, LANES), 1)
    cq2 = jnp.sum(jnp.where(lane == h, ccol_ref[...], 0.0), axis=-1, keepdims=True) * LOG2E
    causal = (lax.broadcasted_iota(jnp.int32, (tq, tq), 0)
              >= lax.broadcasted_iota(jnp.int32, (tq, tq), 1))

    for j in range(seq // tq):
        lo = j * tq
        live = slice(lo, seq)
        kj = k_ref[lo:lo + tq, :].astype(BF16)
        vj = jnp.concatenate([v_ref[lo:lo + tq, :].astype(BF16), ones], axis=-1)
        s = lax.dot_general(qb[live], kj, (((1,), (1,)), ((), ())), preferred_element_type=F32)
        a = s * c1 - crow_ref[pl.ds(h, 1), lo:lo + tq] * LOG2E
        diag = jnp.where(causal, a[:tq], NEG_INF)
        a = diag if lo + tq == seq else jnp.concatenate([diag, a[tq:]], axis=0)
        groups = _lane_groups(a)
        m_blk = jnp.max(functools.reduce(jnp.maximum, groups), axis=-1, keepdims=True) + cq2[live]
        if j == 0:
            m_new = jnp.broadcast_to(m_blk, (seq, LANES))
        else:
            m_prev = m_scr[live, :]
            m_new = jnp.maximum(m_prev, m_blk)
            alpha = jnp.exp2(m_prev - m_new)
        shift = m_new - cq2[live]
        p = jnp.concatenate([jnp.exp2(g - shift).astype(BF16) for g in groups], axis=-1)
        pv = jnp.dot(p, vj, preferred_element_type=F32)
        m_scr[live, :] = m_new
        if j == 0:
            acc_scr[...] = pv
        else:
            acc_scr[live, :] = jnp.concatenate([alpha, alpha], axis=-1) * acc_scr[live, :] + pv

    z = z_ref[...]
    o_ref[...] = ((acc_scr[:, :HEAD_DIM] / acc_scr[:, HEAD_DIM:])
                  * (z * jax.nn.sigmoid(z))).astype(o_ref.dtype)


N_RET_IN, N_FOX_IN, N_RET_SCR, N_FOX_SCR = 11, 4, 7, 2


def _mixers_kernel(*refs, tq):
    ret_in, refs = refs[:N_RET_IN], refs[N_RET_IN:]
    fox_in, refs = refs[:N_FOX_IN], refs[N_FOX_IN:]
    (f_ref, bias_ref, y_ret_ref, y_fox_ref), refs = refs[:4], refs[4:]
    ret_scr, refs = refs[:N_RET_SCR], refs[N_RET_SCR:]
    fox_scr, (ccol_scr, crow_scr) = refs[:N_FOX_SCR], refs[N_FOX_SCR:]

    @pl.when(pl.program_id(1) == 0)
    def _():
        _forget_cumsum_body(f_ref, bias_ref, ccol_scr, crow_scr)

    retention = _retention_body(*ret_in, y_ret_ref, *ret_scr)
    next(retention)
    _fox_body(*fox_in, ccol_scr, crow_scr, y_fox_ref, *fox_scr, tq=tq)
    next(retention, None)


def _mixers(proj, tables, gn_g, f_logits, bias, batch, seq, tq):
    cos, sin, decay, xi, zeta, gch = tables
    m = proj.shape[0]
    head_block = lambda off: pl.BlockSpec((seq, HEAD_DIM), lambda b, h: (b, off + h))
    per_head = lambda shape: pl.BlockSpec((1,) + shape, lambda b, h: (h, 0, 0))
    table = pl.BlockSpec((seq, HEAD_DIM), lambda b, h: (0, 0))
    y_spec = pl.BlockSpec((seq, HEAD_DIM), lambda b, h: (b, h))
    y_shape = jax.ShapeDtypeStruct((m, WIDTH), BF16)
    return pl.pallas_call(
        functools.partial(_mixers_kernel, tq=tq),
        grid=(batch, N_HEADS),
        in_specs=[
            head_block(COL_RET), head_block(COL_RET + N_HEADS),
            head_block(COL_RET + 2 * N_HEADS), head_block(COL_RET + 3 * N_HEADS),
            table, table,
            per_head((CHUNK, CHUNK)), per_head((CHUNK, HEAD_DIM)),
            per_head((CHUNK, HEAD_DIM)), per_head((8, HEAD_DIM)), per_head((1, HEAD_DIM)),
            head_block(COL_FOX), head_block(COL_FOX + N_HEADS),
            head_block(COL_FOX + 2 * N_HEADS), head_block(COL_FOX + 3 * N_HEADS),
            pl.BlockSpec((seq, LANES), lambda b, h: (b, 0)),
            pl.BlockSpec((1, LANES), lambda b, h: (0, 0)),
        ],
        out_specs=[y_spec, y_spec],
        out_shape=[y_shape, y_shape],
        scratch_shapes=([pltpu.VMEM((seq, HEAD_DIM), BF16)] * 6 + [pltpu.VMEM((seq, HEAD_DIM), F32)]
                        + [pltpu.VMEM((seq, LANES), F32), pltpu.VMEM((seq, 2 * HEAD_DIM), F32)]
                        + [pltpu.VMEM((seq, LANES), F32), pltpu.VMEM((LANES, seq), F32)]),
        compiler_params=_params(("parallel", "arbitrary")),
        name="mixers",
    )(proj, proj, proj, proj, cos, sin, decay, xi, zeta, gch, gn_g.reshape(N_HEADS, 1, HEAD_DIM),
      proj, proj, proj, proj, f_logits, bias)


POOL_HALO = 16


def _pool_group(gi, u_ref, halo_ref, z_ref, w_ref, s_ref, tile_in_seq):
    tm = u_ref.shape[0]
    w = POOL_WINDOWS[gi]
    cols = slice(gi * POOL_GROUP_DIM, (gi + 1) * POOL_GROUP_DIM)
    halo = jnp.where(tile_in_seq == 0, 0.0, halo_ref[:, cols])
    t1 = (tile_in_seq * tm + lax.broadcasted_iota(jnp.int32, (tm, LANES), 0) + 1).astype(F32)
    u = u_ref[:, cols]
    total, k = jnp.concatenate([halo, u], axis=0), 1
    while k < w:
        total = total + pltpu.roll(total, k, 0)
        k *= 2
    inv = 1.0 / jnp.minimum(t1, float(w))
    mean = jnp.concatenate([x * inv for x in _lane_groups(total[POOL_HALO:])], axis=-1)
    mixed = jnp.dot((mean - u).astype(BF16), w_ref[gi], preferred_element_type=F32) * s_ref[gi]
    z = z_ref[:, cols]
    return (mixed * (z * jax.nn.sigmoid(z))).astype(BF16)


def _merge_kernel(x_ref, yr_ref, yf_ref, u_ref, halo_ref, pz_ref, ga_ref, gb_ref, gc_ref,
                  pw_ref, ps_ref, wr_ref, wf_ref, wp_ref, wo_ref, fg_ref, o_ref,
                  *, tiles_per_seq, final_norm):
    tile_in_seq = pl.program_id(0) % tiles_per_seq
    quarter = o_ref.shape[1] // 4
    pool = functools.partial(_pool_group, u_ref=u_ref, halo_ref=halo_ref, z_ref=pz_ref,
                             w_ref=pw_ref, s_ref=ps_ref, tile_in_seq=tile_in_seq)

    yr = yr_ref[...]
    br, yp = [], []
    for part in range(4):
        cols = slice(part * quarter, (part + 1) * quarter)
        br.append(jnp.dot(yr, wr_ref[:, cols], preferred_element_type=F32))
        yp.append(pool(part))
    merged = (jax.nn.sigmoid(ga_ref[...]) * jnp.concatenate(br, axis=-1)
              + jax.nn.sigmoid(gb_ref[...]) * jnp.dot(yf_ref[...], wf_ref[...], preferred_element_type=F32)
              + jax.nn.sigmoid(gc_ref[...]) * jnp.dot(jnp.concatenate(yp, axis=-1), wp_ref[...],
                                                      preferred_element_type=F32))
    x = x_ref[...] + jnp.dot(merged.astype(BF16), wo_ref[...], preferred_element_type=F32)
    if final_norm:
        x = x * lax.rsqrt(jnp.mean(x * x, axis=-1, keepdims=True) + EPS) * fg_ref[...]
    o_ref[...] = x


def _merge(x2, y_ret, y_fox, proj, pool_w, pool_scale, w_r, w_f, w_p, w_o, final_g,
           layer, seq, final_norm, tm=256):
    m, d = x2.shape
    n_groups = len(POOL_WINDOWS)
    halo_per_tile = tm // POOL_HALO
    row = lambda width: pl.BlockSpec((tm, width), lambda i: (i, 0))
    col = lambda width, off: pl.BlockSpec((tm, width), lambda i: (i, off * LANES // width))
    once = lambda shape, index: pl.BlockSpec(shape, index, pipeline_mode=pl.Buffered(1))
    weight = lambda rows: once((None, rows, d), lambda i: (layer, 0, 0))
    halo = pl.BlockSpec((POOL_HALO, WIDTH),
                        lambda i: (jnp.maximum(i * halo_per_tile - 1, 0), COL_POOL_U * LANES // WIDTH))
    return pl.pallas_call(
        functools.partial(_merge_kernel, tiles_per_seq=seq // tm, final_norm=final_norm),
        grid=(m // tm,),
        in_specs=[
            row(d), row(WIDTH), row(WIDTH),
            col(WIDTH, COL_POOL_U), halo, col(WIDTH, COL_POOL_Z),
            col(d, COL_GATES), col(d, COL_GATES + d // LANES), col(d, COL_GATES + 2 * d // LANES),
            once((None, n_groups, POOL_GROUP_DIM, POOL_GROUP_DIM), lambda i: (layer, 0, 0, 0)),
            once((None, n_groups, 1, POOL_GROUP_DIM), lambda i: (layer, 0, 0, 0)),
            weight(WIDTH), weight(WIDTH), weight(WIDTH), weight(d),
            once((1, d), lambda i: (0, 0)),
        ],
        out_specs=row(d),
        out_shape=jax.ShapeDtypeStruct((m, d), F32),
        compiler_params=_params(("parallel",)),
        name="merge_out",
    )(x2, y_ret, y_fox, proj, proj, proj, proj, proj, proj,
      pool_w, pool_scale, w_r, w_f, w_p, w_o, final_g)


def kernel(x, norm_g, w_in, ret_gn_g, fox_b_f, pool_w, pool_scale,
           w_ret_branch, w_fox_branch, w_pool_branch, w_out, final_g):
    batch, seq, d = x.shape
    depth = w_in.shape[0]
    tables = _retention_tables(seq)
    n_before = 8 * WIDTH
    w_in_t = jnp.swapaxes(w_in, 1, 2)
    w_main = _prep_w_in(w_in_t, 0)
    w_forget = jnp.pad(w_in_t[:, n_before:n_before + N_HEADS, :],
                       ((0, 0), (0, LANES - N_HEADS), (0, 0))).astype(BF16)
    branch_weights = (w_ret_branch, w_fox_branch, w_pool_branch, w_out)
    pool_w_b = pool_w.astype(BF16)
    pool_scale_r = pool_scale.reshape(depth, len(POOL_WINDOWS), 1, POOL_GROUP_DIM)
    x2 = x.reshape(batch * seq, d)
    for layer in range(depth):
        bias = jnp.pad(fox_b_f[layer], (0, LANES - N_HEADS)).reshape(1, LANES)

        to_bf16 = [t.reshape(-1, d) for t in branch_weights] if layer == 0 else ()
        proj, f_logits, w_main, casts = _inproj(x2, norm_g[layer].reshape(1, d), w_main, w_forget,
                                                w_in_t, layer, to_bf16)
        if layer == 0:
            w_r, w_f, w_p, w_o = (c.reshape(t.shape) for c, t in zip(casts, branch_weights))
        y_ret, y_fox = _mixers(proj, tables, ret_gn_g[layer], f_logits, bias, batch, seq, FOX_BLOCK)
        x2 = _merge(x2, y_ret, y_fox, proj, pool_w_b, pool_scale_r, w_r, w_f, w_p, w_o,
                    final_g.reshape(1, d), layer, seq, final_norm=(layer == depth - 1))
    return x2.reshape(batch, seq, d)
```

```python
import functools

import jax
import jax.numpy as jnp
import numpy as np
from jax import lax
from jax.experimental import pallas as pl
from jax.experimental.pallas import tpu as pltpu

D_MODEL = 2048
HEAD_DIM = 128
N_HEADS = 8
WIDTH = D_MODEL // 2
POOL_WINDOWS = (2, 4, 8, 16)
POOL_GROUP_DIM = WIDTH // len(POOL_WINDOWS)
CHUNK = 128
ROPE_THETA = 10000.0
EPS = 1e-6
NEG_INF = -1e30
LOG2E = 1.4426950408889634
FOX_BLOCK = 512

LANES = 128
N_MAIN = 10 * WIDTH + 3 * D_MODEL
COL_RET = 0
COL_FOX = 4 * N_HEADS
COL_POOL_U = 8 * N_HEADS
COL_POOL_Z = 9 * N_HEADS
COL_GATES = 10 * N_HEADS

BF16 = jnp.bfloat16
F32 = jnp.float32
NT_DIMS = (((1,), (1,)), ((), ()))

VMEM_LIMIT = 56 * 1024 * 1024


def _params(sem, vmem=VMEM_LIMIT):
    return pltpu.CompilerParams(dimension_semantics=sem, vmem_limit_bytes=vmem)


def _inproj_kernel(x_hbm, g_ref, w_ref, wf_ref, *rest, n_casts):
    cast_srcs, rest = rest[:n_casts], rest[n_casts:]
    (o_ref, f_ref), rest = rest[:2], rest[2:]
    cast_dsts, (x_buf, h_scr, x_sem) = rest[:n_casts], rest[n_casts:]
    i, j = pl.program_id(0), pl.program_id(1)
    tm = x_buf.shape[0]

    def x_copy(tile):
        return pltpu.make_async_copy(x_hbm.at[pl.ds(tile * tm, tm), :], x_buf, x_sem)

    @pl.when((i == 0) & (j == 0))
    def _():
        x_copy(0).start()

    @pl.when(j == 0)
    def _():
        x_copy(i).wait()
        x = x_buf[...]
        y = x * lax.rsqrt(jnp.mean(x * x, axis=-1, keepdims=True) + EPS)
        h = (y * g_ref[...]).astype(BF16)
        h_scr[...] = h
        f_ref[...] = lax.dot_general(h, wf_ref[...], NT_DIMS, preferred_element_type=F32)

    @pl.when((j == 1) & (i + 1 < pl.num_programs(0)))
    def _():
        x_copy(i + 1).start()

    h = h_scr[...]
    half = o_ref.shape[1] // 2
    o_ref[:, :half] = lax.dot_general(h, w_ref[:half, :], NT_DIMS, preferred_element_type=F32)
    for src, dst in zip(cast_srcs, cast_dsts):
        dst[...] = src[...].reshape(dst.shape).astype(BF16)
    o_ref[:, half:] = lax.dot_general(h, w_ref[half:, :], NT_DIMS, preferred_element_type=F32)


def _w_rows_spec(layer, rows, step_of):
    def index(*ids):
        first = step_of(*ids) * rows
        return layer, pl.multiple_of(first + jnp.where(first >= 8 * WIDTH, N_HEADS, 0), N_HEADS), 0
    return pl.BlockSpec((pl.Element(1), pl.Element(rows), pl.Element(D_MODEL)), index)


def _wprep_kernel(src_ref, out_ref):
    out_ref[...] = src_ref[0].astype(BF16)


def _prep_w_in(w_in_t, layer, tr=512):
    d = w_in_t.shape[2]
    return pl.pallas_call(
        _wprep_kernel,
        grid=(N_MAIN // tr,),
        in_specs=[_w_rows_spec(layer, tr, lambda i: i)],
        out_specs=pl.BlockSpec((tr, d), lambda i: (i, 0)),
        out_shape=jax.ShapeDtypeStruct((N_MAIN, d), BF16),
        compiler_params=_params(("parallel",)),
        name="w_in_prep",
    )(w_in_t)


def _inproj(x2, g, w_main_t, w_f_t, w_in_t, layer, to_bf16=(), tm=1024, tn=2048):
    m, d = x2.shape
    n = w_main_t.shape[0]
    grid = (m // tm, n // tn)
    n_steps = grid[0] * grid[1]
    step = lambda i, j: i * grid[1] + j
    prep_next = layer + 1 < w_in_t.shape[0]
    in_specs = [
        pl.BlockSpec(memory_space=pl.ANY),
        pl.BlockSpec((1, d), lambda i, j: (0, 0)),
        pl.BlockSpec((tn, d), lambda i, j: (j, 0)),
        pl.BlockSpec((None, LANES, d), lambda i, j: (layer, 0, 0)),
    ]
    out_specs = [
        pl.BlockSpec((tm, tn), lambda i, j: (i, j)),
        pl.BlockSpec((tm, LANES), lambda i, j: (i, 0)),
    ]
    out_shape = [jax.ShapeDtypeStruct((m, n), F32), jax.ShapeDtypeStruct((m, LANES), F32)]
    operands = [x2, g, w_main_t, w_f_t]
    row_block = lambda rows: pl.BlockSpec((rows, d), lambda i, j: (step(i, j), 0))
    if prep_next:
        in_specs.append(_w_rows_spec(layer + 1, N_MAIN // n_steps, step))
        out_specs.append(row_block(N_MAIN // n_steps))
        out_shape.append(jax.ShapeDtypeStruct((N_MAIN, d), BF16))
        operands.append(w_in_t)
    for t in to_bf16:
        in_specs.append(row_block(t.shape[0] // n_steps))
        out_specs.append(row_block(t.shape[0] // n_steps))
        out_shape.append(jax.ShapeDtypeStruct(t.shape, BF16))
        operands.append(t)
    outs = pl.pallas_call(
        functools.partial(_inproj_kernel, n_casts=len(operands) - 4),
        grid=grid,
        in_specs=in_specs,
        out_specs=out_specs,
        out_shape=out_shape,
        scratch_shapes=[pltpu.VMEM((tm, d), F32), pltpu.VMEM((tm, d), BF16),
                        pltpu.SemaphoreType.DMA(())],
        compiler_params=_params(("arbitrary", "arbitrary")),
        name="inproj",
    )(*operands)
    proj, f_logits, *casts = outs
    w_next = casts.pop(0) if prep_next else None
    return proj, f_logits, w_next, casts


def _rotate(t, cos, sin_signed):
    return t * cos + pltpu.roll(t, HEAD_DIM // 2, 1) * sin_signed


def _retention_body(q_ref, k_ref, v_ref, z_ref, cos_ref, sin_ref, decay_ref,
                    xi_ref, zeta_ref, gch_ref, gn_ref, o_ref,
                    q_scr, k_scr, kz_scr, v_scr, inner_scr, state_scr, out_scr):
    decay = decay_ref[0]
    xi = xi_ref[0]
    zeta = zeta_ref[0]
    g_chunk = gch_ref[0, 0:1, :]
    n_chunks = q_ref.shape[0] // CHUNK
    chunk_rows = [pl.ds(c * CHUNK, CHUNK) for c in range(n_chunks)]
    nt = (((1,), (1,)), ((), ()))
    tn = (((0,), (0,)), ((), ()))

    cos = cos_ref[...]
    sin = sin_ref[...]
    q_scr[...] = _rotate(q_ref[...], cos, sin).astype(BF16)
    k = _rotate(k_ref[...], cos, sin) * (HEAD_DIM ** -0.5)
    k_scr[...] = k.astype(BF16)
    v_scr[...] = v_ref[...].astype(BF16)
    for c, rows in enumerate(chunk_rows):
        kz_scr[rows, :] = (k[c * CHUNK:(c + 1) * CHUNK] * zeta).astype(BF16)
    yield

    increments = [lax.dot_general(kz_scr[rows, :], v_scr[rows, :], tn, preferred_element_type=F32)
                  for rows in chunk_rows]
    for rows in chunk_rows:
        inner = lax.dot_general(q_scr[rows, :], k_scr[rows, :], nt,
                                preferred_element_type=F32) * decay
        inner_scr[rows, :] = inner.astype(BF16)

    state = jnp.zeros((HEAD_DIM, HEAD_DIM), F32)
    for rows, inc in zip(chunk_rows, increments):
        state_scr[rows, :] = state.astype(BF16)
        state = g_chunk * state + inc

    for rows in chunk_rows:
        out_scr[rows, :] = (
            jnp.dot(inner_scr[rows, :], v_scr[rows, :], preferred_element_type=F32)
            + jnp.dot(q_scr[rows, :], state_scr[rows, :], preferred_element_type=F32) * xi)

    out = out_scr[...]
    mu = jnp.mean(out, axis=-1, keepdims=True)
    dev = out - mu
    var = jnp.mean(dev * dev, axis=-1, keepdims=True)
    yn = dev * lax.rsqrt(var + EPS) * gn_ref[0]
    z = z_ref[...]
    o_ref[...] = (yn * (z * jax.nn.sigmoid(z))).astype(o_ref.dtype)


def _retention_tables(seq):
    half = HEAD_DIM // 2
    inv = ROPE_THETA ** (-np.arange(half, dtype=np.float64) / half)
    ang = np.arange(seq, dtype=np.float64)[:, None] * inv[None, :]
    cos = np.cos(ang)
    sin = np.sin(ang)
    cos2 = np.concatenate([cos, cos], axis=-1)
    sin2 = np.concatenate([-sin, sin], axis=-1)
    log_g = np.log1p(-np.exp2(-5.0 - np.arange(N_HEADS, dtype=np.float64)))
    idx = np.arange(CHUNK, dtype=np.float64)
    diff = idx[:, None] - idx[None, :]
    decay = np.where(diff >= 0, np.exp(log_g[:, None, None] * np.maximum(diff, 0.0)), 0.0)
    xi = np.exp(log_g[:, None] * (idx + 1.0))
    zeta = np.exp(log_g[:, None] * (CHUNK - 1.0 - idx))
    g_chunk = np.exp(log_g * CHUNK)
    bcast = lambda t: np.broadcast_to(t[:, :, None], (N_HEADS, CHUNK, HEAD_DIM))
    gch = np.broadcast_to(g_chunk[:, None, None], (N_HEADS, 8, HEAD_DIM))
    tables = (cos2, sin2, decay, bcast(xi), bcast(zeta), gch)
    return tuple(jnp.asarray(np.ascontiguousarray(t), dtype=F32) for t in tables)


def _split3(x):
    hi = x.astype(BF16)
    r1 = x - hi.astype(F32)
    mid = r1.astype(BF16)
    lo = (r1 - mid.astype(F32)).astype(BF16)
    return hi, mid, lo


def _forget_cumsum_body(f_ref, b_ref, col_ref, row_ref):
    n_blocks = f_ref.shape[0] // CHUNK
    r = lax.broadcasted_iota(jnp.int32, (CHUNK, CHUNK), 0)
    c = lax.broadcasted_iota(jnp.int32, (CHUNK, CHUNK), 1)
    tri = jnp.where(r >= c, 1.0, 0.0).astype(BF16)
    carry = jnp.zeros((1, LANES), F32)
    for blk in range(n_blocks):
        rows = pl.ds(blk * CHUNK, CHUNK)
        x = f_ref[rows, :] + b_ref[...]
        ls = jnp.minimum(x, 0.0) - jnp.log1p(jnp.exp(-jnp.abs(x)))
        hi, mid, lo = _split3(ls)
        cs = (jnp.dot(tri, hi, preferred_element_type=F32)
              + jnp.dot(tri, mid, preferred_element_type=F32)
              + jnp.dot(tri, lo, preferred_element_type=F32)) + carry
        col_ref[rows, :] = cs
        row_ref[:, rows] = cs.T
        carry = cs[CHUNK - 1:CHUNK, :]


def _lane_groups(x):
    return [x[:, g * LANES:(g + 1) * LANES] for g in range(x.shape[1] // LANES)]


def _fox_body(q_ref, k_ref, v_ref, z_ref, ccol_ref, crow_ref, o_ref,
              m_scr, acc_scr, *, tq):
    h = pl.program_id(1)
    seq = q_ref.shape[0]
    c1 = (HEAD_DIM ** -0.5) * LOG2E
    ones = jnp.ones((tq, HEAD_DIM), BF16)

    qb = q_ref[...].astype(BF16)
    lane = lax.broadcasted_iota(jnp.int32, (seq, LANES), 1)
    cq2 = jnp.sum(jnp.where(lane == h, ccol_ref[...], 0.0), axis=-1, keepdims=True) * LOG2E
    causal = (lax.broadcasted_iota(jnp.int32, (tq, tq), 0)
              >= lax.broadcasted_iota(jnp.int32, (tq, tq), 1))

    for j in range(seq // tq):
        lo = j * tq
        live = slice(lo, seq)
        kj = k_ref[lo:lo + tq, :].astype(BF16)
        vj = jnp.concatenate([v_ref[lo:lo + tq, :].astype(BF16), ones], axis=-1)
        s = lax.dot_general(qb[live], kj, (((1,), (1,)), ((), ())), preferred_element_type=F32)
        a = s * c1 - crow_ref[pl.ds(h, 1), lo:lo + tq] * LOG2E
        diag = jnp.where(causal, a[:tq], NEG_INF)
        a = diag if lo + tq == seq else jnp.concatenate([diag, a[tq:]], axis=0)
        groups = _lane_groups(a)
        m_blk = jnp.max(functools.reduce(jnp.maximum, groups), axis=-1, keepdims=True) + cq2[live]
        if j == 0:
            m_new = jnp.broadcast_to(m_blk, (seq, LANES))
        else:
            m_prev = m_scr[live, :]
            m_new = jnp.maximum(m_prev, m_blk)
            alpha = jnp.exp2(m_prev - m_new)
        shift = m_new - cq2[live]
        p = jnp.concatenate([jnp.exp2(g - shift).astype(BF16) for g in groups], axis=-1)
        pv = jnp.dot(p, vj, preferred_element_type=F32)
        m_scr[live, :] = m_new
        if j == 0:
            acc_scr[...] = pv
        else:
            acc_scr[live, :] = jnp.concatenate([alpha, alpha], axis=-1) * acc_scr[live, :] + pv

    z = z_ref[...]
    o_ref[...] = ((acc_scr[:, :HEAD_DIM] / acc_scr[:, HEAD_DIM:])
                  * (z * jax.nn.sigmoid(z))).astype(o_ref.dtype)


N_RET_IN, N_FOX_IN, N_RET_SCR, N_FOX_SCR = 11, 4, 7, 2


def _mixers_kernel(*refs, tq):
    ret_in, refs = refs[:N_RET_IN], refs[N_RET_IN:]
    fox_in, refs = refs[:N_FOX_IN], refs[N_FOX_IN:]
    (f_ref, bias_ref, y_ret_ref, y_fox_ref), refs = refs[:4], refs[4:]
    ret_scr, refs = refs[:N_RET_SCR], refs[N_RET_SCR:]
    fox_scr, (ccol_scr, crow_scr) = refs[:N_FOX_SCR], refs[N_FOX_SCR:]

    @pl.when(pl.program_id(1) == 0)
    def _():
        _forget_cumsum_body(f_ref, bias_ref, ccol_scr, crow_scr)

    retention = _retention_body(*ret_in, y_ret_ref, *ret_scr)
    next(retention)
    _fox_body(*fox_in, ccol_scr, crow_scr, y_fox_ref, *fox_scr, tq=tq)
    next(retention, None)


def _mixers(proj, tables, gn_g, f_logits, bias, batch, seq, tq):
    cos, sin, decay, xi, zeta, gch = tables
    m = proj.shape[0]
    head_block = lambda off: pl.BlockSpec((seq, HEAD_DIM), lambda b, h: (b, off + h))
    per_head = lambda shape: pl.BlockSpec((1,) + shape, lambda b, h: (h, 0, 0))
    table = pl.BlockSpec((seq, HEAD_DIM), lambda b, h: (0, 0))
    y_spec = pl.BlockSpec((seq, HEAD_DIM), lambda b, h: (b, h))
    y_shape = jax.ShapeDtypeStruct((m, WIDTH), BF16)
    return pl.pallas_call(
        functools.partial(_mixers_kernel, tq=tq),
        grid=(batch, N_HEADS),
        in_specs=[
            head_block(COL_RET), head_block(COL_RET + N_HEADS),
            head_block(COL_RET + 2 * N_HEADS), head_block(COL_RET + 3 * N_HEADS),
            table, table,
            per_head((CHUNK, CHUNK)), per_head((CHUNK, HEAD_DIM)),
            per_head((CHUNK, HEAD_DIM)), per_head((8, HEAD_DIM)), per_head((1, HEAD_DIM)),
            head_block(COL_FOX), head_block(COL_FOX + N_HEADS),
            head_block(COL_FOX + 2 * N_HEADS), head_block(COL_FOX + 3 * N_HEADS),
            pl.BlockSpec((seq, LANES), lambda b, h: (b, 0)),
            pl.BlockSpec((1, LANES), lambda b, h: (0, 0)),
        ],
        out_specs=[y_spec, y_spec],
        out_shape=[y_shape, y_shape],
        scratch_shapes=([pltpu.VMEM((seq, HEAD_DIM), BF16)] * 6 + [pltpu.VMEM((seq, HEAD_DIM), F32)]
                        + [pltpu.VMEM((seq, LANES), F32), pltpu.VMEM((seq, 2 * HEAD_DIM), F32)]
                        + [pltpu.VMEM((seq, LANES), F32), pltpu.VMEM((LANES, seq), F32)]),
        compiler_params=_params(("parallel", "arbitrary")),
        name="mixers",
    )(proj, proj, proj, proj, cos, sin, decay, xi, zeta, gch, gn_g.reshape(N_HEADS, 1, HEAD_DIM),
      proj, proj, proj, proj, f_logits, bias)


POOL_HALO = 16


def _pool_group(gi, u_ref, halo_ref, z_ref, w_ref, s_ref, tile_in_seq):
    tm = u_ref.shape[0]
    w = POOL_WINDOWS[gi]
    cols = slice(gi * POOL_GROUP_DIM, (gi + 1) * POOL_GROUP_DIM)
    halo = jnp.where(tile_in_seq == 0, 0.0, halo_ref[:, cols])
    t1 = (tile_in_seq * tm + lax.broadcasted_iota(jnp.int32, (tm, LANES), 0) + 1).astype(F32)
    u = u_ref[:, cols]
    total, k = jnp.concatenate([halo, u], axis=0), 1
    while k < w:
        total = total + pltpu.roll(total, k, 0)
        k *= 2
    inv = 1.0 / jnp.minimum(t1, float(w))
    mean = jnp.concatenate([x * inv for x in _lane_groups(total[POOL_HALO:])], axis=-1)
    mixed = jnp.dot((mean - u).astype(BF16), w_ref[gi], preferred_element_type=F32) * s_ref[gi]
    z = z_ref[:, cols]
    return (mixed * (z * jax.nn.sigmoid(z))).astype(BF16)


def _merge_kernel(x_ref, yr_ref, yf_ref, u_ref, halo_ref, pz_ref, ga_ref, gb_ref, gc_ref,
                  pw_ref, ps_ref, wr_ref, wf_ref, wp_ref, wo_ref, fg_ref, o_ref,
                  *, tiles_per_seq, final_norm):
    tile_in_seq = pl.program_id(0) % tiles_per_seq
    quarter = o_ref.shape[1] // 4
    pool = functools.partial(_pool_group, u_ref=u_ref, halo_ref=halo_ref, z_ref=pz_ref,
                             w_ref=pw_ref, s_ref=ps_ref, tile_in_seq=tile_in_seq)

    yr = yr_ref[...]
    br, yp = [], []
    for part in range(4):
        cols = slice(part * quarter, (part + 1) * quarter)
        br.append(jnp.dot(yr, wr_ref[:, cols], preferred_element_type=F32))
        yp.append(pool(part))
    merged = (jax.nn.sigmoid(ga_ref[...]) * jnp.concatenate(br, axis=-1)
              + jax.nn.sigmoid(gb_ref[...]) * jnp.dot(yf_ref[...], wf_ref[...], preferred_element_type=F32)
              + jax.nn.sigmoid(gc_ref[...]) * jnp.dot(jnp.concatenate(yp, axis=-1), wp_ref[...],
                                                      preferred_element_type=F32))
    x = x_ref[...] + jnp.dot(merged.astype(BF16), wo_ref[...], preferred_element_type=F32)
    if final_norm:
        x = x * lax.rsqrt(jnp.mean(x * x, axis=-1, keepdims=True) + EPS) * fg_ref[...]
    o_ref[...] = x


def _merge(x2, y_ret, y_fox, proj, pool_w, pool_scale, w_r, w_f, w_p, w_o, final_g,
           layer, seq, final_norm, tm=256):
    m, d = x2.shape
    n_groups = len(POOL_WINDOWS)
    halo_per_tile = tm // POOL_HALO
    row = lambda width: pl.BlockSpec((tm, width), lambda i: (i, 0))
    col = lambda width, off: pl.BlockSpec((tm, width), lambda i: (i, off * LANES // width))
    once = lambda shape, index: pl.BlockSpec(shape, index, pipeline_mode=pl.Buffered(1))
    weight = lambda rows: once((None, rows, d), lambda i: (layer, 0, 0))
    halo = pl.BlockSpec((POOL_HALO, WIDTH),
                        lambda i: (jnp.maximum(i * halo_per_tile - 1, 0), COL_POOL_U * LANES // WIDTH))
    return pl.pallas_call(
        functools.partial(_merge_kernel, tiles_per_seq=seq // tm, final_norm=final_norm),
        grid=(m // tm,),
        in_specs=[
            row(d), row(WIDTH), row(WIDTH),
            col(WIDTH, COL_POOL_U), halo, col(WIDTH, COL_POOL_Z),
            col(d, COL_GATES), col(d, COL_GATES + d // LANES), col(d, COL_GATES + 2 * d // LANES),
            once((None, n_groups, POOL_GROUP_DIM, POOL_GROUP_DIM), lambda i: (layer, 0, 0, 0)),
            once((None, n_groups, 1, POOL_GROUP_DIM), lambda i: (layer, 0, 0, 0)),
            weight(WIDTH), weight(WIDTH), weight(WIDTH), weight(d),
            once((1, d), lambda i: (0, 0)),
        ],
        out_specs=row(d),
        out_shape=jax.ShapeDtypeStruct((m, d), F32),
        compiler_params=_params(("parallel",)),
        name="merge_out",
    )(x2, y_ret, y_fox, proj, proj, proj, proj, proj, proj,
      pool_w, pool_scale, w_r, w_f, w_p, w_o, final_g)


def kernel(x, norm_g, w_in, ret_gn_g, fox_b_f, pool_w, pool_scale,
           w_ret_branch, w_fox_branch, w_pool_branch, w_out, final_g):
    batch, seq, d = x.shape
    depth = w_in.shape[0]
    tables = _retention_tables(seq)
    n_before = 8 * WIDTH
    w_in_t = jnp.swapaxes(w_in, 1, 2)
    w_main = _prep_w_in(w_in_t, 0)
    w_forget = jnp.pad(w_in_t[:, n_before:n_before + N_HEADS, :],
                       ((0, 0), (0, LANES - N_HEADS), (0, 0))).astype(BF16)
    branch_weights = (w_ret_branch, w_fox_branch, w_pool_branch, w_out)
    pool_w_b = pool_w.astype(BF16)
    pool_scale_r = pool_scale.reshape(depth, len(POOL_WINDOWS), 1, POOL_GROUP_DIM)
    x2 = x.reshape(batch * seq, d)
    for layer in range(depth):
        bias = jnp.pad(fox_b_f[layer], (0, LANES - N_HEADS)).reshape(1, LANES)

        to_bf16 = [t.reshape(-1, d) for t in branch_weights] if layer == 0 else ()
        proj, f_logits, w_main, casts = _inproj(x2, norm_g[layer].reshape(1, d), w_main, w_forget,
                                                w_in_t, layer, to_bf16)
        if layer == 0:
            w_r, w_f, w_p, w_o = (c.reshape(t.shape) for c, t in zip(casts, branch_weights))
        y_ret, y_fox = _mixers(proj, tables, ret_gn_g[layer], f_logits, bias, batch, seq, FOX_BLOCK)
        x2 = _merge(x2, y_ret, y_fox, proj, pool_w_b, pool_scale_r, w_r, w_f, w_p, w_o,
                    final_g.reshape(1, d), layer, seq, final_norm=(layer == depth - 1))
    return x2.reshape(batch, seq, d)
```
